```python
import jax, jax.numpy as jnp
from jax import lax
import numpy as np

D_MODEL = 1024
BATCH = 16
SEQ = 4096
DEPTH = 1
DEC_BATCH = 8
DEC_SEQ = 64
PAST_LEN = 2048

CHUNK = 64
N_META = 16
D_MIX = D_MODEL
RW_HEADS = 8
RW_HEAD = 64
RW_WIDTH = RW_HEADS * RW_HEAD
W_LORA = 64
A_LORA = 64
G_LORA = 128
RW_COLS = 3 * RW_WIDTH + W_LORA + A_LORA + G_LORA
MLA_HEADS = 8
NOPE = 64
ROPE = 32
VDIM = 64
Q_LORA = 256
KV_LORA = 128
MLA_COLS = Q_LORA + KV_LORA + ROPE
N_IN = RW_COLS + MLA_COLS
D_FF = 2816
Q_BLOCK = 128
ROPE_THETA = 10000.0
NORM_EPS = 1e-6
GN_EPS = 64e-5
MLA_SCALE = (NOPE + ROPE) ** -0.5

kernel_name = "hymba_rwkv7_mla_macaron_stream"


def rmsnorm(x, g):
    xf = x.astype(jnp.float32)
    y = xf * lax.rsqrt(jnp.mean(xf * xf, axis=-1, keepdims=True) + NORM_EPS)
    return (y * g.astype(jnp.float32)).astype(x.dtype)


def swiglu(h, w1, w3, w2):
    return (jax.nn.silu(h @ w1) * (h @ w3)) @ w2


def rope(x, pos):
    half = ROPE // 2
    inv = ROPE_THETA ** (-jnp.arange(half, dtype=jnp.float32) / half)
    ang = pos.astype(jnp.float32)[:, None] * inv[None, :]
    shape = (1, pos.shape[0]) + (1,) * (x.ndim - 3) + (half,)
    cos = jnp.cos(ang).reshape(shape)
    sin = jnp.sin(ang).reshape(shape)
    xf = x.astype(jnp.float32)
    x1, x2 = xf[..., :half], xf[..., half:]
    return jnp.concatenate([x1 * cos - x2 * sin, x1 * sin + x2 * cos], axis=-1).astype(x.dtype)


def chunk_id(pos):
    return jnp.where(pos < N_META, -1, (pos - N_META) // CHUNK)


def pre_mix(x, L):
    x = x + 0.5 * swiglu(rmsnorm(x, L["norm_ffn1"]), L["ffn1_w1"], L["ffn1_w3"], L["ffn1_w2"])
    p = rmsnorm(x, L["norm_mix"]) @ L["w_in"]
    return x, p


def rwkv_scan(S0, r, w, k, v, kk, kka):
    def step(S, xs):
        r_t, w_t, k_t, v_t, kk_t, kka_t = xs
        sa = jnp.einsum("bhvk,bhk->bhv", S, -kk_t)
        S = S * w_t[:, :, None, :] + sa[..., None] * kka_t[:, :, None, :] + v_t[..., :, None] * k_t[..., None, :]
        return S, jnp.einsum("bhvk,bhk->bhv", S, r_t)
    xs = tuple(jnp.moveaxis(t, 1, 0) for t in (r, w, k, v, kk, kka))
    S, ys = lax.scan(step, S0, xs)
    return S, jnp.moveaxis(ys, 0, 1)


def rwkv_mix(s, wkv0, L):
    B, T, _ = s.shape
    f = lambda name: L[name].astype(jnp.float32)
    sf = s.astype(jnp.float32)
    cuts = [RW_WIDTH, 2 * RW_WIDTH, 3 * RW_WIDTH, 3 * RW_WIDTH + W_LORA, 3 * RW_WIDTH + W_LORA + A_LORA]
    r, k, v, wl, al, gl = jnp.split(sf, cuts, axis=-1)
    logw = -jax.nn.softplus(-(f("w0") + jnp.tanh(wl) @ f("w_w2"))) - 0.5
    decay = jnp.exp(-jnp.exp(logw))
    a = jax.nn.sigmoid(f("a0") + al @ f("w_a2"))
    g = jax.nn.sigmoid(gl) @ f("w_g2")
    hd = lambda t: t.reshape(B, T, RW_HEADS, RW_HEAD)
    kk = hd(k * f("k_k"))
    kk = kk / jnp.maximum(jnp.sqrt(jnp.sum(kk * kk, axis=-1, keepdims=True)), 1e-12)
    k = k * (1.0 + (a - 1.0) * f("k_a"))
    r, k, v, decay, a = hd(r), hd(k), hd(v), hd(decay), hd(a)
    S, y = rwkv_scan(wkv0.astype(jnp.float32), r, decay, k, v, kk, kk * a)
    mu = jnp.mean(y, axis=-1, keepdims=True)
    var = jnp.mean(jnp.square(y - mu), axis=-1, keepdims=True)
    y = ((y - mu) * lax.rsqrt(var + GN_EPS)).reshape(B, T, RW_WIDTH) * f("ln_x_w") + f("ln_x_b")
    y = y + (jnp.sum(r * k * f("r_k"), axis=-1, keepdims=True) * v).reshape(B, T, RW_WIDTH)
    return (y * g).astype(s.dtype), S.astype(wkv0.dtype)


def mla_kv(p, pos, L):
    o = RW_COLS + Q_LORA
    c = rmsnorm(p[..., o:o + KV_LORA], L["kv_norm"])
    kr = rope(p[..., o + KV_LORA:o + KV_LORA + ROPE], pos)
    return c, kr


def mla_q(p, pos, L):
    cq = rmsnorm(p[..., RW_COLS:RW_COLS + Q_LORA], L["q_norm"])
    q = jnp.einsum("btr,rhd->bthd", cq, L["w_q_up"])
    q_r = rope(q[..., NOPE:], pos)
    q_lat = jnp.einsum("bthd,chd->bthc", q[..., :NOPE], L["w_uk"])
    return q_lat, q_r


def attend(q_lat, q_r, c, kr, mask):
    s = (jnp.einsum("bqhc,bkc->bhqk", q_lat, c) + jnp.einsum("bqhr,bkr->bhqk", q_r, kr)).astype(jnp.float32) * MLA_SCALE
    if mask is not None:
        s = jnp.where(mask, s, -1e30)
    pr = jax.nn.softmax(s, axis=-1).astype(c.dtype)
    return jnp.einsum("bhqk,bkc->bqhc", pr, c)


def prompt_attention(q_lat, q_r, c, kr):
    B, T = q_lat.shape[0], q_lat.shape[1]
    nb = -(-T // Q_BLOCK)
    pad = nb * Q_BLOCK - T
    def blocks(t):
        t = jnp.pad(t, ((0, 0), (0, pad), (0, 0), (0, 0)))
        return jnp.moveaxis(t.reshape((B, nb, Q_BLOCK) + t.shape[2:]), 1, 0)
    cid_k = chunk_id(jnp.arange(T))
    def one(args):
        ql, qr, i = args
        cid_q = chunk_id(i * Q_BLOCK + jnp.arange(Q_BLOCK))
        return attend(ql, qr, c, kr, cid_q[:, None] >= cid_k[None, :])
    out = lax.map(one, (blocks(q_lat), blocks(q_r), jnp.arange(nb)))
    return jnp.moveaxis(out, 0, 1).reshape(B, nb * Q_BLOCK, MLA_HEADS, KV_LORA)[:, :T]


def layer(x, L, pos, shift_prev, wkv0, prefix):
    B, T, _ = x.shape
    x, p = pre_mix(x, L)
    p_rw = p[..., :RW_COLS]
    prev = jnp.concatenate([shift_prev.astype(p.dtype), p_rw[:, :-1]], axis=1)
    rw_out, wkv = rwkv_mix(p_rw + L["mu_shift"] * (prev - p_rw), wkv0, L)
    c, kr = mla_kv(p, pos, L)
    q_lat, q_r = mla_q(p, pos, L)
    if prefix is None:
        lat = prompt_attention(q_lat, q_r, c, kr)
    else:
        c_all = jnp.concatenate([prefix[0].astype(c.dtype), c], axis=1)
        kr_all = jnp.concatenate([prefix[1].astype(kr.dtype), kr], axis=1)
        lat = attend(q_lat, q_r, c_all, kr_all, None)
    mla_out = jnp.einsum("bqhc,chd->bqhd", lat, L["w_uv"]).reshape(B, T, MLA_HEADS * VDIM)
    x = x + jnp.concatenate([rw_out, mla_out], axis=-1) @ L["w_out"]
    x = x + 0.5 * swiglu(rmsnorm(x, L["norm_ffn2"]), L["ffn2_w1"], L["ffn2_w3"], L["ffn2_w2"])
    return x, p_rw[:, -1:], wkv, c, kr


def setup_inputs(seed: int = 0) -> dict:
    key = jax.random.key(seed)
    ks = list(jax.random.split(key, 40))
    cnt = [0]
    def nk():
        cnt[0] += 1
        return ks[cnt[0] - 1]
    f32 = jnp.float32
    def nrm(shape, scale=1.0):
        return jax.random.normal(nk(), shape, f32) * scale
    def gain(shape):
        return 1.0 + nrm(shape, 0.01)
    return {
        "x_prompt": nrm((BATCH, SEQ, D_MODEL)),
        "x_sample": nrm((DEC_BATCH, DEC_SEQ, D_MODEL)),
        "cache_ckv": nrm((DEPTH, DEC_BATCH, PAST_LEN, KV_LORA)),
        "cache_krope": nrm((DEPTH, DEC_BATCH, PAST_LEN, ROPE)),
        "state_wkv": nrm((DEPTH, DEC_BATCH, RW_HEADS, RW_HEAD, RW_HEAD), 0.3),
        "state_shift": nrm((DEPTH, DEC_BATCH, 1, RW_COLS)),
        "meta_tokens": nrm((N_META, D_MODEL)),
        "norm_ffn1": gain((DEPTH, D_MODEL)),
        "ffn1_w1": nrm((DEPTH, D_MODEL, D_FF), D_MODEL ** -0.5),
        "ffn1_w3": nrm((DEPTH, D_MODEL, D_FF), D_MODEL ** -0.5),
        "ffn1_w2": nrm((DEPTH, D_FF, D_MODEL), D_FF ** -0.5),
        "norm_mix": gain((DEPTH, D_MODEL)),
        "w_in": nrm((DEPTH, D_MODEL, N_IN), D_MODEL ** -0.5),
        "mu_shift": jax.random.uniform(nk(), (DEPTH, RW_COLS), f32),
        "w0": nrm((DEPTH, RW_WIDTH), 0.5),
        "w_w2": nrm((DEPTH, W_LORA, RW_WIDTH), W_LORA ** -0.5),
        "a0": nrm((DEPTH, RW_WIDTH), 0.1),
        "w_a2": nrm((DEPTH, A_LORA, RW_WIDTH), 0.5 * A_LORA ** -0.5),
        "w_g2": nrm((DEPTH, G_LORA, RW_WIDTH), G_LORA ** -0.5),
        "k_k": 1.0 + nrm((DEPTH, RW_WIDTH), 0.1),
        "k_a": 1.0 + nrm((DEPTH, RW_WIDTH), 0.1),
        "r_k": nrm((DEPTH, RW_HEADS, RW_HEAD), 0.1),
        "ln_x_w": gain((DEPTH, RW_WIDTH)),
        "ln_x_b": nrm((DEPTH, RW_WIDTH), 0.01),
        "q_norm": gain((DEPTH, Q_LORA)),
        "w_q_up": nrm((DEPTH, Q_LORA, MLA_HEADS, NOPE + ROPE), Q_LORA ** -0.5),
        "kv_norm": gain((DEPTH, KV_LORA)),
        "w_uk": nrm((DEPTH, KV_LORA, MLA_HEADS, NOPE), KV_LORA ** -0.5),
        "w_uv": nrm((DEPTH, KV_LORA, MLA_HEADS, VDIM), KV_LORA ** -0.5),
        "w_out": nrm((DEPTH, D_MIX, D_MODEL), D_MIX ** -0.5),
        "norm_ffn2": gain((DEPTH, D_MODEL)),
        "ffn2_w1": nrm((DEPTH, D_MODEL, D_FF), D_MODEL ** -0.5),
        "ffn2_w3": nrm((DEPTH, D_MODEL, D_FF), D_MODEL ** -0.5),
        "ffn2_w2": nrm((DEPTH, D_FF, D_MODEL), D_FF ** -0.5),
        "final_norm": gain((D_MODEL,)),
    }


def reference(x_prompt, x_sample, cache_ckv, cache_krope, state_wkv, state_shift, meta_tokens,
              norm_ffn1, ffn1_w1, ffn1_w3, ffn1_w2, norm_mix, w_in, mu_shift, w0, w_w2, a0, w_a2,
              w_g2, k_k, k_a, r_k, ln_x_w, ln_x_b, q_norm, w_q_up, kv_norm, w_uk, w_uv, w_out,
              norm_ffn2, ffn2_w1, ffn2_w3, ffn2_w2, final_norm):
    P = dict(norm_ffn1=norm_ffn1, ffn1_w1=ffn1_w1, ffn1_w3=ffn1_w3, ffn1_w2=ffn1_w2, norm_mix=norm_mix,
             w_in=w_in, mu_shift=mu_shift, w0=w0, w_w2=w_w2, a0=a0, w_a2=w_a2, w_g2=w_g2, k_k=k_k,
             k_a=k_a, r_k=r_k, ln_x_w=ln_x_w, ln_x_b=ln_x_b, q_norm=q_norm, w_q_up=w_q_up,
             kv_norm=kv_norm, w_uk=w_uk, w_uv=w_uv, w_out=w_out, norm_ffn2=norm_ffn2,
             ffn2_w1=ffn2_w1, ffn2_w3=ffn2_w3, ffn2_w2=ffn2_w2)
    B, S_len, _ = x_prompt.shape
    Bd, Tn, _ = x_sample.shape
    past = cache_ckv.shape[2]
    dt = x_prompt.dtype
    meta = meta_tokens.astype(dt)
    x = jnp.concatenate([jnp.broadcast_to(meta[None], (B, N_META, D_MODEL)), x_prompt], axis=1)
    xs = x_sample
    m = meta[None]
    pos_p = jnp.arange(N_META + S_len)
    pos_m = jnp.arange(N_META)
    pos_s = N_META + past + jnp.arange(Tn)
    ckv_p, kr_p, wkv_p, sh_p = [], [], [], []
    ckv_s, kr_s, wkv_s, sh_s = [], [], [], []
    for l in range(DEPTH):
        L = {name: arr[l] for name, arr in P.items()}
        x, sh, wkv, c, kr = layer(x, L, pos_p, jnp.zeros((B, 1, RW_COLS), dt),
                                  jnp.zeros((B, RW_HEADS, RW_HEAD, RW_HEAD), dt), None)
        ckv_p.append(c); kr_p.append(kr); wkv_p.append(wkv); sh_p.append(sh)
        _, p_m = pre_mix(m, L)
        c_m, kr_m = mla_kv(p_m, pos_m, L)
        prefix = (jnp.concatenate([jnp.broadcast_to(c_m, (Bd, N_META, KV_LORA)), cache_ckv[l].astype(c_m.dtype)], axis=1),
                  jnp.concatenate([jnp.broadcast_to(kr_m, (Bd, N_META, ROPE)), cache_krope[l].astype(kr_m.dtype)], axis=1))
        xs, sh2, wkv2, c2, kr2 = layer(xs, L, pos_s, state_shift[l], state_wkv[l], prefix)
        ckv_s.append(c2); kr_s.append(kr2); wkv_s.append(wkv2); sh_s.append(sh2)
        if l + 1 < DEPTH:
            m = layer(m, L, pos_m, jnp.zeros((1, 1, RW_COLS), dt),
                      jnp.zeros((1, RW_HEADS, RW_HEAD, RW_HEAD), dt), None)[0]
    y_prompt = rmsnorm(x, final_norm)[:, N_META:]
    y_sample = rmsnorm(xs, final_norm)
    return (y_prompt, y_sample,
            jnp.stack(ckv_p), jnp.stack(kr_p), jnp.stack(wkv_p), jnp.stack(sh_p),
            jnp.stack(ckv_s), jnp.stack(kr_s), jnp.stack(wkv_s), jnp.stack(sh_s))
```

```python
import functools

import jax
import jax.numpy as jnp
from jax import lax
from jax.experimental import pallas as pl
from jax.experimental.pallas import tpu as pltpu

F32 = jnp.float32
BF16 = jnp.bfloat16

D_MODEL = 1024
D_FF = 2816
CHUNK = 64
N_META = 16
HEADS = 8
HD = 64
RW = HEADS * HD
ROPE = 32
Q_LORA = 256
KV_LORA = 128
NORM_EPS = 1e-6
GN_EPS = 64e-5
MLA_SCALE = (HD + ROPE) ** -0.5
ROPE_THETA = 10000.0

LANE = 128
SCAN = 64
PAIR = 2 * HD
N_PAIR = HEADS // 2
QK_W = 256

O_R, O_K, O_V, O_WL, O_AL, O_GL = 0, 512, 1024, 1536, 1664, 1792
RWP = 1920
O_Q, O_KV, O_KR, O_KRR = 1920, 2176, 2304, 2432
PW = 2560
FF_CHUNK = 1408
VMEM_LIMIT = 56 * 1024 * 1024
NEG = -1e30


def _dot(a, b):
    return jnp.dot(a.astype(BF16), b.astype(BF16), preferred_element_type=F32)


def _dot_nt(a, b):
    return lax.dot_general(a.astype(BF16), b.astype(BF16), (((1,), (1,)), ((), ())),
                           preferred_element_type=F32)


def _split2(x):
    hi = x.astype(BF16)
    lo = (x - hi.astype(F32)).astype(BF16)
    return hi, lo


def _dot_x2(a, b):
    hi, lo = _split2(a)
    bb = b.astype(BF16)
    return (jnp.dot(hi, bb, preferred_element_type=F32) + jnp.dot(lo, bb, preferred_element_type=F32))


def _rms(x, g):
    return x * lax.rsqrt(jnp.mean(x * x, axis=-1, keepdims=True) + NORM_EPS) * g


def _ffn(h, w1_ref, w3_ref, w2_ref):
    acc = None
    for c in range(D_FF // FF_CHUNK):
        sl = slice(c * FF_CHUNK, (c + 1) * FF_CHUNK)
        a = jnp.dot(h, w1_ref[:, sl], preferred_element_type=F32)
        b = jnp.dot(h, w3_ref[:, sl], preferred_element_type=F32)
        z = (a * jax.nn.sigmoid(a) * b).astype(BF16)
        t = jnp.dot(z, w2_ref[sl, :], preferred_element_type=F32)
        acc = t if acc is None else acc + t
    return acc


def _stage_a_kernel(x_ref, sh0_ref, cos_ref, sin_ref,
                    n1_ref, w1_ref, w3_ref, w2_ref, nmix_ref, win_ref, mu_ref,
                    w0_ref, ww2_ref, a0_ref, wa2_ref, wg2_ref, kk_ref, ka_ref, rk_ref, ones_ref,
                    qn_ref, wqn_ref, wqr_ref, wqrr_ref, kvn_ref, wuk_ref,
                    x1_o, r_o, lw_o, k_o, v_o, a_o, b_o, bon_o, g_o, c_o, kr_o, kcat_o, q_o, sh_o,
                    prev_s, *, tm):
    t = pl.program_id(1)
    x = x_ref[0]
    h = _rms(x, n1_ref[...]).astype(BF16)
    x1 = x + 0.5 * _ffn(h, w1_ref, w3_ref, w2_ref)
    x1_o[0] = x1
    hm = _rms(x1, nmix_ref[...]).astype(BF16)
    p = jnp.dot(hm, win_ref[...], preferred_element_type=F32)

    prw = p[:, :RWP]

    @pl.when(t == 0)
    def _():
        prev_s[...] = sh0_ref[0]

    prev_row = prev_s[...]
    rolled = pltpu.roll(prw, 1, 0)
    row = lax.broadcasted_iota(jnp.int32, prw.shape, 0)
    prev = jnp.where(row == 0, prev_row, rolled)
    last = prw[tm - 1:tm, :]
    prev_s[...] = last
    sh_o[0] = last
    s = prw + mu_ref[...] * (prev - prw)

    r = s[:, O_R:O_R + RW]
    k = s[:, O_K:O_K + RW]
    v = s[:, O_V:O_V + RW]
    wl = s[:, O_WL:O_WL + LANE]
    al = s[:, O_AL:O_AL + LANE]
    gl = s[:, O_GL:O_GL + LANE]
    z = w0_ref[...] + _dot(jnp.tanh(wl), ww2_ref[...])
    softplus = jnp.maximum(-z, 0.0) + jnp.log(1.0 + jnp.exp(-jnp.abs(z)))
    logw = -softplus - 0.5
    lw_o[0] = -jnp.exp(logw)
    a_sig = jax.nn.sigmoid(a0_ref[...] + _dot(al, wa2_ref[...]))
    g_o[0] = _dot(jax.nn.sigmoid(gl), wg2_ref[...])
    kk = k * kk_ref[...]
    ss = _dot_x2(kk * kk, ones_ref[...])
    kk = kk / jnp.maximum(jnp.sqrt(ss), 1e-12)
    k2 = k * (1.0 + (a_sig - 1.0) * ka_ref[...])
    r_o[0] = r
    k_o[0] = k2
    v_o[0] = v
    a_o[0] = -kk
    b_o[0] = kk * a_sig
    bon_o[0] = _dot_x2(r * k2 * rk_ref[...], ones_ref[...]) * v

    cos = cos_ref[...]
    sin = sin_ref[...]
    c = _rms(p[:, O_KV:O_KV + KV_LORA], kvn_ref[...])
    kr = p[:, O_KR:O_KR + LANE] * cos + p[:, O_KRR:O_KRR + LANE] * sin
    c_o[0] = c
    kr_o[0] = kr
    kcat_o[0] = jnp.concatenate([c, kr], axis=1).astype(BF16)
    cq = _rms(p[:, O_Q:O_Q + Q_LORA], qn_ref[...]).astype(BF16)
    qn = jnp.dot(cq, wqn_ref[...], preferred_element_type=F32)
    qlat = _dot(qn, wuk_ref[...])
    qr = jnp.dot(cq, wqr_ref[...], preferred_element_type=F32)
    qrr = jnp.dot(cq, wqrr_ref[...], preferred_element_type=F32)
    for hh in range(HEADS):
        sl = slice(hh * LANE, (hh + 1) * LANE)
        qrh = qr[:, sl] * cos + qrr[:, sl] * sin
        q_o[0, hh] = (jnp.concatenate([qlat[:, sl], qrh], axis=1) * MLA_SCALE).astype(BF16)


def _full(shape):
    return pl.BlockSpec(memory_space=pltpu.VMEM)


def _stage_a(x, sh0, cos, sin, W, tm):
    B, T, _ = x.shape
    nt = T // tm
    tok = lambda w: pl.BlockSpec((1, tm, w), lambda b, t: (b, t, 0))
    weights = [W["n1"], W["f1w1"], W["f1w3"], W["f1w2"], W["nmix"], W["win"], W["mu"],
               W["w0"], W["ww2"], W["a0"], W["wa2"], W["wg2"], W["kk"], W["ka"], W["rk"], W["ones"],
               W["qn"], W["wqn"], W["wqr"], W["wqrr"], W["kvn"], W["wuk"]]
    in_specs = ([tok(D_MODEL), pl.BlockSpec((1, 1, RWP), lambda b, t: (b, 0, 0)),
                 pl.BlockSpec((tm, LANE), lambda b, t: (t, 0)), pl.BlockSpec((tm, LANE), lambda b, t: (t, 0))]
                + [_full(w.shape) for w in weights])
    f = lambda w, dt=F32: jax.ShapeDtypeStruct((B, T, w), dt)
    out_shape = [f(D_MODEL)] + [f(RW)] * 8 + [f(KV_LORA), f(LANE), f(QK_W, BF16),
                                             jax.ShapeDtypeStruct((B, HEADS, T, QK_W), BF16),
                                             jax.ShapeDtypeStruct((B, 1, RWP), F32)]
    out_specs = ([tok(D_MODEL)] + [tok(RW)] * 8 + [tok(KV_LORA), tok(LANE), tok(QK_W),
                                                  pl.BlockSpec((1, HEADS, tm, QK_W), lambda b, t: (b, 0, t, 0)),
                                                  pl.BlockSpec((1, 1, RWP), lambda b, t: (b, 0, 0))])
    outs = pl.pallas_call(
        functools.partial(_stage_a_kernel, tm=tm),
        grid=(B, nt), in_specs=in_specs, out_specs=out_specs, out_shape=out_shape,
        scratch_shapes=[pltpu.VMEM((1, RWP), F32)],
        compiler_params=pltpu.CompilerParams(dimension_semantics=("arbitrary", "arbitrary"),
                                             vmem_limit_bytes=VMEM_LIMIT),
        name="stage_a",
    )(x, sh0, cos, sin, *weights)
    names = ["x1", "r", "lw", "k", "v", "a", "b", "bon", "g", "c", "kr", "kcat", "q", "shift"]
    return dict(zip(names, outs))


def _stack(x, lo_mask):
    return jnp.concatenate([jnp.where(lo_mask, x, 0.0), jnp.where(lo_mask, 0.0, x)], axis=0)


def _rwkv_kernel(r_ref, lw_ref, k_ref, v_ref, a_ref, b_ref, s0_ref, y_o, st_o, st_s):
    c = pl.program_id(1)

    @pl.when(c == 0)
    def _():
        st_s[...] = s0_ref[0]

    n2 = 2 * SCAN
    ti = lax.broadcasted_iota(jnp.int32, (SCAN, SCAN), 0)
    tj = lax.broadcasted_iota(jnp.int32, (SCAN, SCAN), 1)
    tri = jnp.where(ti >= tj, 1.0, 0.0).astype(BF16)
    ri = lax.broadcasted_iota(jnp.int32, (n2, n2), 0) % SCAN
    ci = lax.broadcasted_iota(jnp.int32, (n2, n2), 1) % SCAN
    strict = ri > ci
    incl = ri >= ci
    lo_mask = lax.broadcasted_iota(jnp.int32, (SCAN, PAIR), 1) < HD

    lw_all = lw_ref[0]
    l1 = lw_all.astype(BF16)
    d1 = lw_all - l1.astype(F32)
    l2 = d1.astype(BF16)
    l3 = (d1 - l2.astype(F32)).astype(BF16)
    cum_all = (jnp.dot(tri, l1, preferred_element_type=F32) + jnp.dot(tri, l2, preferred_element_type=F32)
               + jnp.dot(tri, l3, preferred_element_type=F32))

    for j in range(N_PAIR):
        sl = slice(j * PAIR, (j + 1) * PAIR)
        lw = lw_all[:, sl]
        cum = cum_all[:, sl]
        cum_c = cum[SCAN - 1:SCAN, :]
        e_p = jnp.exp(cum)
        e_pm = jnp.exp(cum - lw)
        e_n = jnp.exp(-cum)
        e_c = jnp.exp(cum_c - cum)
        p_c = jnp.exp(cum_c)
        r = r_ref[0, :, sl]
        k = k_ref[0, :, sl]
        v = v_ref[0, :, sl]
        a = a_ref[0, :, sl]
        b = b_ref[0, :, sl]
        ls = _stack(a * e_pm, lo_mask).astype(BF16)
        rs = _stack(r * e_p, lo_mask)
        bs = _stack(b * e_n, lo_mask).astype(BF16)
        ks = _stack(k * e_n, lo_mask).astype(BF16)
        vs = _stack(v, lo_mask).astype(BF16)
        bh = _stack(b * e_c, lo_mask).astype(BF16)
        kh = _stack(k * e_c, lo_mask).astype(BF16)

        aa = _dot_nt(jnp.concatenate([ls, rs.astype(BF16)], axis=0), jnp.concatenate([bs, ks], axis=0))
        l_m = jnp.where(strict, aa[:n2, :n2], 0.0)
        ak = jnp.where(strict, aa[:n2, n2:], 0.0)
        rb = jnp.where(incl, aa[n2:, :n2], 0.0)
        rk = jnp.where(incl, aa[n2:, n2:], 0.0)

        xx = jnp.concatenate([ls.astype(F32), _dot(ak, vs)], axis=1)
        pw = l_m
        for it in range(6):
            xx = xx + _dot(pw, xx)
            if it < 5:
                pw = _dot(pw, pw)
        wl = xx[:, :PAIR]
        u0 = xx[:, PAIR:]
        rq = rs + _dot(rb, wl)
        uv = jnp.concatenate([u0, vs.astype(F32)], axis=0)
        y0 = _dot(jnp.concatenate([rb, rk], axis=1), uv)
        g_m = _dot(wl.T, bh)
        h_m = _dot(uv.T, jnp.concatenate([bh, kh], axis=0))

        st = st_s[j]
        ys = _dot_nt(rq, st) + y0
        y_o[0, :, sl] = ys[:SCAN] + ys[SCAN:]
        st_s[j] = p_c * st + _dot_x2(st, g_m) + h_m

    st_o[0] = st_s[...]


def _rwkv(ins, s0):
    B, T, _ = ins[0].shape
    nc = T // SCAN
    tok = pl.BlockSpec((1, SCAN, RW), lambda b, c: (b, c, 0))
    st_spec = pl.BlockSpec((1, N_PAIR, PAIR, PAIR), lambda b, c: (b, 0, 0, 0))
    return pl.pallas_call(
        _rwkv_kernel,
        grid=(B, nc), in_specs=[tok] * 6 + [st_spec], out_specs=[tok, st_spec],
        out_shape=[jax.ShapeDtypeStruct((B, T, RW), F32), jax.ShapeDtypeStruct((B, N_PAIR, PAIR, PAIR), F32)],
        scratch_shapes=[pltpu.VMEM((N_PAIR, PAIR, PAIR), F32)],
        compiler_params=pltpu.CompilerParams(dimension_semantics=("arbitrary", "arbitrary"),
                                             vmem_limit_bytes=VMEM_LIMIT),
        name="rwkv_scan",
    )(*ins, s0)


def _attn_kernel(q_ref, k_ref, wuv_ref, o_ref, m_s, l_s, acc_s, *, tq, tk, n_main, n_valid_extra, n_extra_blocks):
    i = pl.program_id(1)
    rows = HEADS * tq
    q = q_ref[0].reshape(rows, QK_W)
    m_s[...] = jnp.full(m_s.shape, NEG, F32)
    l_s[...] = jnp.zeros(l_s.shape, F32)
    acc_s[...] = jnp.zeros(acc_s.shape, F32)

    def block(start, mask):
        kb = k_ref[0, pl.ds(start, tk), :]
        s = lax.dot_general(q, kb, (((1,), (1,)), ((), ())), preferred_element_type=F32)
        if mask is not None:
            s = jnp.where(mask, s, NEG)
        m_prev = m_s[...]
        m_next = jnp.maximum(m_prev, jnp.max(s, axis=1, keepdims=True))
        alpha = jnp.exp(m_prev - m_next)
        pr = jnp.exp(s - jnp.concatenate([m_next] * (tk // LANE), axis=1))
        l_s[...] = alpha * l_s[...] + jnp.sum(pr, axis=1, keepdims=True)
        acc_s[...] = alpha * acc_s[...] + jnp.dot(pr.astype(BF16), kb[:, :KV_LORA], preferred_element_type=F32)
        m_s[...] = m_next

    if n_main:
        def body(kb_i, carry):
            block(pl.multiple_of(kb_i * tk, tk), None)
            return carry

        lax.fori_loop(0, i, body, 0)
        rc = (lax.broadcasted_iota(jnp.int32, (rows, tk), 0) % tq) // CHUNK
        cc = lax.broadcasted_iota(jnp.int32, (rows, tk), 1) // CHUNK
        block(pl.multiple_of(i * tk, tk), cc <= rc)
    for e in range(n_extra_blocks):
        valid = min(tk, n_valid_extra - e * tk)
        mask = None if valid == tk else lax.broadcasted_iota(jnp.int32, (rows, tk), 1) < valid
        block(n_main + e * tk, mask)

    lat = acc_s[...] / l_s[...]
    out = None
    for hh in range(HEADS):
        t = jnp.dot(lat[hh * tq:(hh + 1) * tq].astype(BF16), wuv_ref[hh], preferred_element_type=F32)
        out = t if out is None else out + t
    o_ref[0] = out


def _attention(q, keys, wuv, tq, tk, n_main, n_valid_extra):
    B, _, T, _ = q.shape
    tkeys = keys.shape[1]
    n_extra_blocks = (tkeys - n_main) // tk
    kern = functools.partial(_attn_kernel, tq=tq, tk=tk, n_main=n_main, n_valid_extra=n_valid_extra,
                             n_extra_blocks=n_extra_blocks)
    rows = HEADS * tq
    return pl.pallas_call(
        kern,
        grid=(B, T // tq),
        in_specs=[pl.BlockSpec((1, HEADS, tq, QK_W), lambda b, i: (b, 0, i, 0)),
                  pl.BlockSpec((1, tkeys, QK_W), lambda b, i: (b, 0, 0)),
                  _full(wuv.shape)],
        out_specs=pl.BlockSpec((1, tq, RW), lambda b, i: (b, i, 0)),
        out_shape=jax.ShapeDtypeStruct((B, T, RW), F32),
        scratch_shapes=[pltpu.VMEM((rows, LANE), F32)] * 3,
        compiler_params=pltpu.CompilerParams(dimension_semantics=("arbitrary", "arbitrary"),
                                             vmem_limit_bytes=VMEM_LIMIT),
        name="mla_attention",
    )(q, keys, wuv)


def _stage_d_kernel(x1_ref, y_ref, bon_ref, g_ref, mla_ref, ones_ref, lnw_ref, lnb_ref, wo_ref,
                    n2_ref, w1_ref, w3_ref, w2_ref, fn_ref, o_ref):
    y = y_ref[...]
    mu = _dot_x2(y, ones_ref[...]) * (1.0 / HD)
    d = y - mu
    var = _dot_x2(d * d, ones_ref[...]) * (1.0 / HD)
    yn = d * lax.rsqrt(var + GN_EPS) * lnw_ref[...] + lnb_ref[...]
    rw = ((yn + bon_ref[...]) * g_ref[...]).astype(BF16)
    x2 = (x1_ref[...] + jnp.dot(rw, wo_ref[:RW, :], preferred_element_type=F32)
          + jnp.dot(mla_ref[...].astype(BF16), wo_ref[RW:, :], preferred_element_type=F32))
    h = _rms(x2, n2_ref[...]).astype(BF16)
    x3 = x2 + 0.5 * _ffn(h, w1_ref, w3_ref, w2_ref)
    o_ref[...] = _rms(x3, fn_ref[...])


def _stage_d(x1, y, bon, g, mla, W, tm):
    B, T, _ = x1.shape
    n = B * T
    flat = lambda t: t.reshape(n, t.shape[-1])
    tok = lambda w: pl.BlockSpec((tm, w), lambda i: (i, 0))
    weights = [W["ones"], W["lnw"], W["lnb"], W["wout"], W["n2"], W["f2w1"], W["f2w3"], W["f2w2"], W["fnorm"]]
    out = pl.pallas_call(
        _stage_d_kernel,
        grid=(n // tm,),
        in_specs=[tok(D_MODEL), tok(RW), tok(RW), tok(RW), tok(RW)] + [_full(w.shape) for w in weights],
        out_specs=tok(D_MODEL),
        out_shape=jax.ShapeDtypeStruct((n, D_MODEL), F32),
        compiler_params=pltpu.CompilerParams(dimension_semantics=("arbitrary",), vmem_limit_bytes=VMEM_LIMIT),
        name="stage_d",
    )(flat(x1), flat(y), flat(bon), flat(g), flat(mla), *weights)
    return out.reshape(B, T, D_MODEL)


def _pad_rw_cols(t):
    z = jnp.zeros(t.shape[:-1] + (HD,), t.dtype)
    return jnp.concatenate([t[..., :1536], t[..., 1536:1600], z, t[..., 1600:1664], z, t[..., 1664:1792]], axis=-1)


def _unpad_rw_cols(t):
    return jnp.concatenate([t[..., :1536], t[..., O_WL:O_WL + 64], t[..., O_AL:O_AL + 64], t[..., O_GL:O_GL + LANE]],
                           axis=-1)


def _rot_cols(w):
    half = ROPE // 2
    return jnp.concatenate([-w[..., half:], w[..., :half]], axis=-1)


def _prep_weights(norm_ffn1, ffn1_w1, ffn1_w3, ffn1_w2, norm_mix, w_in, mu_shift, w0, w_w2, a0, w_a2, w_g2, k_k,
                  k_a, r_k, ln_x_w, ln_x_b, q_norm, w_q_up, kv_norm, w_uk, w_uv, w_out, norm_ffn2, ffn2_w1,
                  ffn2_w3, ffn2_w2, final_norm):
    row = lambda t: t.reshape(1, -1).astype(F32)
    zc = lambda n: jnp.zeros((D_MODEL, n), F32)
    wi = w_in[0]
    wkr = wi[:, 2176:2208]
    win = jnp.concatenate([_pad_rw_cols(wi[:, :1792]), wi[:, 1792:2176], wkr, zc(LANE - ROPE), _rot_cols(wkr),
                           zc(LANE - ROPE)], axis=1)
    zl = jnp.zeros((HD, RW), F32)
    wq = w_q_up[0]
    wq_rope = wq[:, :, HD:]
    pad_heads = lambda t: jnp.pad(t, ((0, 0), (0, 0), (0, LANE - ROPE))).reshape(Q_LORA, HEADS * LANE)
    eye = jnp.eye(HEADS, dtype=F32)
    wuk_bd = jnp.einsum("chd,hg->hdgc", w_uk[0], eye).reshape(HEADS * HD, HEADS * KV_LORA)
    wuv_pad = jnp.einsum("chd,hg->hcgd", w_uv[0], eye).reshape(HEADS, KV_LORA, RW)
    ones_bd = jnp.kron(eye, jnp.ones((HD, HD), F32))
    return dict(
        n1=row(norm_ffn1), f1w1=ffn1_w1[0].astype(BF16), f1w3=ffn1_w3[0].astype(BF16), f1w2=ffn1_w2[0].astype(BF16),
        nmix=row(norm_mix), win=win.astype(BF16), mu=_pad_rw_cols(row(mu_shift)),
        w0=row(w0), ww2=jnp.concatenate([w_w2[0], zl], axis=0).astype(BF16), a0=row(a0),
        wa2=jnp.concatenate([w_a2[0], zl], axis=0).astype(BF16), wg2=w_g2[0].astype(BF16),
        kk=row(k_k), ka=row(k_a), rk=row(r_k), ones=ones_bd.astype(BF16),
        qn=row(q_norm), wqn=wq[:, :, :HD].reshape(Q_LORA, RW).astype(BF16),
        wqr=pad_heads(wq_rope).astype(BF16), wqrr=pad_heads(_rot_cols(wq_rope)).astype(BF16),
        kvn=row(kv_norm), wuk=wuk_bd.astype(BF16), wuv=wuv_pad.astype(BF16),
        lnw=row(ln_x_w), lnb=row(ln_x_b), wout=w_out[0].astype(BF16),
        n2=row(norm_ffn2), f2w1=ffn2_w1[0].astype(BF16), f2w3=ffn2_w3[0].astype(BF16),
        f2w2=ffn2_w2[0].astype(BF16), fnorm=row(final_norm))


def _rope_tables(pos):
    half = ROPE // 2
    inv = ROPE_THETA ** (-jnp.arange(half, dtype=F32) / half)
    ang = pos.astype(F32)[:, None] * inv[None, :]
    pad = lambda t: jnp.pad(jnp.concatenate([t, t], axis=1), ((0, 0), (0, LANE - ROPE)))
    return pad(jnp.cos(ang)), pad(jnp.sin(ang))


def _state_to_pairs(s):
    B = s.shape[0]
    s = s.reshape(B, N_PAIR, 2, HD, HD)
    z = jnp.zeros_like(s[:, :, 0])
    top = jnp.concatenate([s[:, :, 0], z], axis=-1)
    bot = jnp.concatenate([z, s[:, :, 1]], axis=-1)
    return jnp.concatenate([top, bot], axis=-2)


def _pairs_to_state(s):
    B = s.shape[0]
    return jnp.stack([s[:, :, :HD, :HD], s[:, :, HD:, HD:]], axis=2).reshape(B, HEADS, HD, HD)


def _pad_rows(t, n):
    return jnp.pad(t, ((0, 0), (0, n - t.shape[1]), (0, 0)))


def _tile(n, pref):
    return pref if n % pref == 0 else n


def kernel(x_prompt, x_sample, cache_ckv, cache_krope, state_wkv, state_shift, meta_tokens, norm_ffn1, ffn1_w1, ffn1_w3, ffn1_w2, norm_mix, w_in, mu_shift, w0, w_w2, a0, w_a2, w_g2, k_k, k_a, r_k, ln_x_w, ln_x_b, q_norm, w_q_up, kv_norm, w_uk, w_uv, w_out, norm_ffn2, ffn2_w1, ffn2_w3, ffn2_w2, final_norm):
    assert w_in.shape[0] == 1, "single-layer stack"
    B, S, _ = x_prompt.shape
    Bd, Tn, _ = x_sample.shape
    past = cache_ckv.shape[2]
    assert S % SCAN == 0 and Tn % SCAN == 0
    W = _prep_weights(norm_ffn1, ffn1_w1, ffn1_w3, ffn1_w2, norm_mix, w_in, mu_shift, w0, w_w2, a0, w_a2, w_g2,
                      k_k, k_a, r_k, ln_x_w, ln_x_b, q_norm, w_q_up, kv_norm, w_uk, w_uv, w_out, norm_ffn2,
                      ffn2_w1, ffn2_w3, ffn2_w2, final_norm)
    scan_keys = ("r", "lw", "k", "v", "a", "b")

    cos_m, sin_m = _rope_tables(jnp.arange(N_META))
    am = _stage_a(meta_tokens[None].astype(F32), jnp.zeros((1, 1, RWP), F32), cos_m, sin_m, W, N_META)
    _, st_m = _rwkv([_pad_rows(am[n], SCAN) for n in scan_keys], jnp.zeros((1, N_PAIR, PAIR, PAIR), F32))

    tm = _tile(S, 256)
    cos_p, sin_p = _rope_tables(N_META + jnp.arange(S))
    ap = _stage_a(x_prompt, jnp.broadcast_to(am["shift"], (B, 1, RWP)), cos_p, sin_p, W, tm)
    y_p, st_p = _rwkv([ap[n] for n in scan_keys], jnp.broadcast_to(st_m, (B, N_PAIR, PAIR, PAIR)))
    tq = _tile(S, 256)
    keys_p = jnp.concatenate([ap["kcat"], jnp.broadcast_to(am["kcat"], (B, N_META, QK_W)),
                              jnp.zeros((B, tq - N_META, QK_W), BF16)], axis=1)
    mla_p = _attention(ap["q"], keys_p, W["wuv"], tq, tq, S, N_META)
    y_prompt = _stage_d(ap["x1"], y_p, ap["bon"], ap["g"], mla_p, W, tm)

    cos_s, sin_s = _rope_tables(N_META + past + jnp.arange(Tn))
    as_ = _stage_a(x_sample, _pad_rw_cols(state_shift[0].astype(F32)), cos_s, sin_s, W, Tn)
    y_s, st_s = _rwkv([as_[n] for n in scan_keys], _state_to_pairs(state_wkv[0].astype(F32)))
    tk_s = 256
    cache_k = jnp.concatenate([cache_ckv[0], cache_krope[0], jnp.zeros((Bd, past, QK_W - KV_LORA - ROPE), F32)],
                              axis=-1).astype(BF16)
    n_valid = past + Tn + N_META
    n_keys = -(-n_valid // tk_s) * tk_s
    keys_s = jnp.concatenate([cache_k, as_["kcat"], jnp.broadcast_to(am["kcat"], (Bd, N_META, QK_W)),
                              jnp.zeros((Bd, n_keys - n_valid, QK_W), BF16)], axis=1)
    mla_s = _attention(as_["q"], keys_s, W["wuv"], Tn, tk_s, 0, n_valid)
    y_sample = _stage_d(as_["x1"], y_s, as_["bon"], as_["g"], mla_s, W, Tn)

    bc = lambda t, n: jnp.broadcast_to(t, (n,) + t.shape[1:])
    ckv_p = jnp.concatenate([bc(am["c"], B), ap["c"]], axis=1)[None]
    kr_p = jnp.concatenate([bc(am["kr"], B), ap["kr"]], axis=1)[None, ..., :ROPE]
    return (y_prompt, y_sample, ckv_p, kr_p, _pairs_to_state(st_p)[None], _unpad_rw_cols(ap["shift"])[None],
            as_["c"][None], as_["kr"][None, ..., :ROPE], _pairs_to_state(st_s)[None], _unpad_rw_cols(as_["shift"])[None])
```

```python
import functools

import jax
import jax.numpy as jnp
from jax import lax
from jax.experimental import pallas as pl
from jax.experimental.pallas import tpu as pltpu

F32 = jnp.float32
BF16 = jnp.bfloat16

D_MODEL = 1024
D_FF = 2816
CHUNK = 64
N_META = 16
HEADS = 8
HD = 64
RW = HEADS * HD
ROPE = 32
Q_LORA = 256
KV_LORA = 128
NORM_EPS = 1e-6
GN_EPS = 64e-5
MLA_SCALE = (HD + ROPE) ** -0.5
ROPE_THETA = 10000.0

LANE = 128
SCAN = 64
PAIR = 2 * HD
N_PAIR = HEADS // 2
RWKV_ROWS = 4
QK_W = 256

O_R, O_K, O_V, O_WL, O_AL, O_GL = 0, 512, 1024, 1536, 1664, 1792
RWP = 1920
O_Q, O_KV, O_KR, O_KRR = 1920, 2176, 2304, 2432
PW = 2560
FF_CHUNK = 1408
VMEM_LIMIT = 56 * 1024 * 1024
NEG = -1e30
LOG2E = 1.4426950408889634
ATT_GROUP = 512


def _dot(a, b):
    return jnp.dot(a.astype(BF16), b.astype(BF16), preferred_element_type=F32)


def _dot_nt(a, b):
    return lax.dot_general(a.astype(BF16), b.astype(BF16), (((1,), (1,)), ((), ())),
                           preferred_element_type=F32)


def _split2(x):
    hi = x.astype(BF16)
    lo = (x - hi.astype(F32)).astype(BF16)
    return hi, lo


def _dot_x2(a, b):
    hi, lo = _split2(a)
    bb = b.astype(BF16)
    return (jnp.dot(hi, bb, preferred_element_type=F32) + jnp.dot(lo, bb, preferred_element_type=F32))


def _rms(x, g):
    return x * lax.rsqrt(jnp.mean(x * x, axis=-1, keepdims=True) + NORM_EPS) * g


def _ffn(h, w1_ref, w3_ref, w2_ref):
    acc = None
    for c in range(D_FF // FF_CHUNK):
        sl = slice(c * FF_CHUNK, (c + 1) * FF_CHUNK)
        a = jnp.dot(h, w1_ref[:, sl], preferred_element_type=F32)
        b = jnp.dot(h, w3_ref[:, sl], preferred_element_type=F32)
        z = (a * jax.nn.sigmoid(a) * b).astype(BF16)
        t = jnp.dot(z, w2_ref[sl, :], preferred_element_type=F32)
        acc = t if acc is None else acc + t
    return acc


def _stage_a_kernel(x_ref, sh0_ref, cos_ref, sin_ref,
                    n1_ref, w1_ref, w3_ref, w2_ref, nmix_ref, win_ref, mu_ref,
                    w0_ref, ww2_ref, a0_ref, wa2_ref, wg2_ref, kk_ref, ka_ref, rk_ref, ones_ref,
                    qn_ref, wqn_ref, wqr_ref, wqrr_ref, kvn_ref, wuk_ref,
                    x1_o, r_o, lw_o, k_o, v_o, a_o, b_o, bon_o, g_o, c_o, kr_o, kcat_o, q_o, sh_o,
                    prev_s, *, tm):
    t = pl.program_id(1)
    x = x_ref[0]
    h = _rms(x, n1_ref[...]).astype(BF16)
    x1 = x + 0.5 * _ffn(h, w1_ref, w3_ref, w2_ref)
    x1_o[0] = x1
    hm = _rms(x1, nmix_ref[...]).astype(BF16)
    p = jnp.dot(hm, win_ref[...], preferred_element_type=F32)

    prw = p[:, :RWP]

    @pl.when(t == 0)
    def _():
        prev_s[...] = sh0_ref[0]

    prev_row = prev_s[...]
    rolled = pltpu.roll(prw, 1, 0)
    row = lax.broadcasted_iota(jnp.int32, prw.shape, 0)
    prev = jnp.where(row == 0, prev_row, rolled)
    last = prw[tm - 1:tm, :]
    prev_s[...] = last
    sh_o[0] = last
    s = prw + mu_ref[...] * (prev - prw)

    r = s[:, O_R:O_R + RW]
    k = s[:, O_K:O_K + RW]
    v = s[:, O_V:O_V + RW]
    wl = s[:, O_WL:O_WL + LANE]
    al = s[:, O_AL:O_AL + LANE]
    gl = s[:, O_GL:O_GL + LANE]
    z = w0_ref[...] + _dot(jnp.tanh(wl), ww2_ref[...])
    softplus = jnp.maximum(-z, 0.0) + jnp.log(1.0 + jnp.exp(-jnp.abs(z)))
    logw = -softplus - 0.5
    lw_o[0] = -jnp.exp(logw)
    a_sig = jax.nn.sigmoid(a0_ref[...] + _dot(al, wa2_ref[...]))
    g_o[0] = _dot(jax.nn.sigmoid(gl), wg2_ref[...])
    kk = k * kk_ref[...]
    ss = _dot(kk * kk, ones_ref[...])
    kk = kk / jnp.maximum(jnp.sqrt(ss), 1e-12)
    k2 = k * (1.0 + (a_sig - 1.0) * ka_ref[...])
    r_o[0] = r
    k_o[0] = k2
    v_o[0] = v
    a_o[0] = -kk
    b_o[0] = kk * a_sig
    bon_o[0] = _dot(r * k2 * rk_ref[...], ones_ref[...]) * v

    cos = cos_ref[...]
    sin = sin_ref[...]
    c = _rms(p[:, O_KV:O_KV + KV_LORA], kvn_ref[...])
    kr = p[:, O_KR:O_KR + LANE] * cos + p[:, O_KRR:O_KRR + LANE] * sin
    c_o[0] = c
    kr_o[0] = kr
    kcat_o[0] = jnp.concatenate([c, kr], axis=1).astype(BF16)
    cq = _rms(p[:, O_Q:O_Q + Q_LORA], qn_ref[...]).astype(BF16)
    qn = jnp.dot(cq, wqn_ref[...], preferred_element_type=F32)
    qlat = _dot(qn, wuk_ref[...])
    qr = jnp.dot(cq, wqr_ref[...], preferred_element_type=F32)
    qrr = jnp.dot(cq, wqrr_ref[...], preferred_element_type=F32)
    for hh in range(HEADS):
        sl = slice(hh * LANE, (hh + 1) * LANE)
        qrh = qr[:, sl] * cos + qrr[:, sl] * sin
        q_o[0, hh] = (jnp.concatenate([qlat[:, sl], qrh], axis=1) * (MLA_SCALE * LOG2E)).astype(BF16)


def _full(shape):
    return pl.BlockSpec(memory_space=pltpu.VMEM)


def _stage_a(x, sh0, cos, sin, W, tm):
    B, T, _ = x.shape
    nt = T // tm
    tok = lambda w: pl.BlockSpec((1, tm, w), lambda b, t: (b, t, 0))
    weights = [W["n1"], W["f1w1"], W["f1w3"], W["f1w2"], W["nmix"], W["win"], W["mu"],
               W["w0"], W["ww2"], W["a0"], W["wa2"], W["wg2"], W["kk"], W["ka"], W["rk"], W["ones"],
               W["qn"], W["wqn"], W["wqr"], W["wqrr"], W["kvn"], W["wuk"]]
    in_specs = ([tok(D_MODEL), pl.BlockSpec((1, 1, RWP), lambda b, t: (b, 0, 0)),
                 pl.BlockSpec((tm, LANE), lambda b, t: (t, 0)), pl.BlockSpec((tm, LANE), lambda b, t: (t, 0))]
                + [_full(w.shape) for w in weights])
    f = lambda w, dt=F32: jax.ShapeDtypeStruct((B, T, w), dt)
    out_shape = [f(D_MODEL)] + [f(RW)] * 8 + [f(KV_LORA), f(LANE), f(QK_W, BF16),
                                             jax.ShapeDtypeStruct((B, HEADS, T, QK_W), BF16),
                                             jax.ShapeDtypeStruct((B, 1, RWP), F32)]
    out_specs = ([tok(D_MODEL)] + [tok(RW)] * 8 + [tok(KV_LORA), tok(LANE), tok(QK_W),
                                                  pl.BlockSpec((1, HEADS, tm, QK_W), lambda b, t: (b, 0, t, 0)),
                                                  pl.BlockSpec((1, 1, RWP), lambda b, t: (b, 0, 0))])
    outs = pl.pallas_call(
        functools.partial(_stage_a_kernel, tm=tm),
        grid=(B, nt), in_specs=in_specs, out_specs=out_specs, out_shape=out_shape,
        scratch_shapes=[pltpu.VMEM((1, RWP), F32)],
        compiler_params=pltpu.CompilerParams(dimension_semantics=("arbitrary", "arbitrary"),
                                             vmem_limit_bytes=VMEM_LIMIT),
        name="stage_a",
    )(x, sh0, cos, sin, *weights)
    names = ["x1", "r", "lw", "k", "v", "a", "b", "bon", "g", "c", "kr", "kcat", "q", "shift"]
    return dict(zip(names, outs))


def _stack(x, lo_mask):
    return jnp.concatenate([jnp.where(lo_mask, x, 0.0), jnp.where(lo_mask, 0.0, x)], axis=0)


def _rwkv_kernel(r_ref, lw_ref, k_ref, v_ref, a_ref, b_ref, s0_ref, y_o, st_o, st_s, *, nb):
    c = pl.program_id(1)

    @pl.when(c == 0)
    def _():
        st_s[...] = s0_ref[...]

    n2 = 2 * SCAN
    ti = lax.broadcasted_iota(jnp.int32, (SCAN, SCAN), 0)
    tj = lax.broadcasted_iota(jnp.int32, (SCAN, SCAN), 1)
    tri = jnp.where(ti >= tj, 1.0, 0.0).astype(BF16)
    ri = lax.broadcasted_iota(jnp.int32, (n2, n2), 0) % SCAN
    ci = lax.broadcasted_iota(jnp.int32, (n2, n2), 1) % SCAN
    strict = ri > ci
    incl = ri >= ci
    lo_mask = lax.broadcasted_iota(jnp.int32, (SCAN, PAIR), 1) < HD

    chains = [(i, j) for i in range(nb) for j in range(N_PAIR)]
    sl = lambda j: slice(j * PAIR, (j + 1) * PAIR)
    sts = [st_s[i, j] for i, j in chains]
    ins = [[ref[i, :, sl(j)] for ref in (r_ref, k_ref, v_ref, a_ref, b_ref)] for i, j in chains]

    cums, lws = [], []
    for i in range(nb):
        lw_all = lw_ref[i]
        l1 = lw_all.astype(BF16)
        d1 = lw_all - l1.astype(F32)
        l2 = d1.astype(BF16)
        l3 = (d1 - l2.astype(F32)).astype(BF16)
        cum_all = (jnp.dot(tri, l1, preferred_element_type=F32) + jnp.dot(tri, l2, preferred_element_type=F32)
                   + jnp.dot(tri, l3, preferred_element_type=F32))
        for j in range(N_PAIR):
            cums.append(cum_all[:, sl(j)])
            lws.append(lw_all[:, sl(j)])

    def prep(n):
        r, k, v, a, b = ins[n]
        cum = cums[n]
        cum_c = cum[SCAN - 1:SCAN, :]
        e_n = jnp.exp(-cum)
        e_c = jnp.exp(cum_c - cum)
        return dict(
            p_c=jnp.exp(cum_c),
            ls=_stack(a * jnp.exp(cum - lws[n]), lo_mask).astype(BF16),
            rs=_stack(r * jnp.exp(cum), lo_mask),
            bs=_stack(b * e_n, lo_mask).astype(BF16), ks=_stack(k * e_n, lo_mask).astype(BF16),
            vs=_stack(v, lo_mask).astype(BF16),
            bh=_stack(b * e_c, lo_mask).astype(BF16), kh=_stack(k * e_c, lo_mask).astype(BF16))

    P = [prep(n) for n in range(len(chains))]
    aas = [_dot_nt(jnp.concatenate([p["ls"], p["rs"].astype(BF16)], axis=0),
                   jnp.concatenate([p["bs"], p["ks"]], axis=0)) for p in P]
    pws = [jnp.where(strict, aa[:n2, :n2], 0.0) for aa in aas]
    aks = [jnp.where(strict, aa[:n2, n2:], 0.0) for aa in aas]
    rbs = [jnp.where(incl, aa[n2:, :n2], 0.0) for aa in aas]
    rks = [jnp.where(incl, aa[n2:, n2:], 0.0) for aa in aas]

    xs = [jnp.concatenate([p["ls"].astype(F32), _dot(ak, p["vs"])], axis=1) for p, ak in zip(P, aks)]
    for it in range(6):
        xs = [x + _dot(pw, x) for x, pw in zip(xs, pws)]
        if it < 5:
            pws = [_dot(pw, pw) for pw in pws]

    ys_out, st_new = [], []
    for n, p in enumerate(P):
        wl = xs[n][:, :PAIR]
        uv = jnp.concatenate([xs[n][:, PAIR:], p["vs"].astype(F32)], axis=0)
        rq = p["rs"] + _dot(rbs[n], wl)
        y0 = _dot(jnp.concatenate([rbs[n], rks[n]], axis=1), uv)
        g_m = _dot(wl.T, p["bh"])
        h_m = _dot(uv.T, jnp.concatenate([p["bh"], p["kh"]], axis=0))
        ys = _dot_nt(rq, sts[n]) + y0
        ys_out.append(ys[:SCAN] + ys[SCAN:])
        st_new.append(p["p_c"] * sts[n] + _dot_x2(sts[n], g_m) + h_m)

    for i in range(nb):
        y_o[i] = jnp.concatenate(ys_out[i * N_PAIR:(i + 1) * N_PAIR], axis=1)
    for n, (i, j) in enumerate(chains):
        st_s[i, j] = st_new[n]
        st_o[i, j] = st_new[n]


def _rwkv(ins, s0):
    B, T, _ = ins[0].shape
    nc = T // SCAN
    nb = RWKV_ROWS if B % RWKV_ROWS == 0 else 1
    tok = pl.BlockSpec((nb, SCAN, RW), lambda b, c: (b, c, 0))
    st_spec = pl.BlockSpec((nb, N_PAIR, PAIR, PAIR), lambda b, c: (b, 0, 0, 0))
    return pl.pallas_call(
        functools.partial(_rwkv_kernel, nb=nb),
        grid=(B // nb, nc), in_specs=[tok] * 6 + [st_spec], out_specs=[tok, st_spec],
        out_shape=[jax.ShapeDtypeStruct((B, T, RW), F32), jax.ShapeDtypeStruct((B, N_PAIR, PAIR, PAIR), F32)],
        scratch_shapes=[pltpu.VMEM((nb, N_PAIR, PAIR, PAIR), F32)],
        compiler_params=pltpu.CompilerParams(dimension_semantics=("arbitrary", "arbitrary"),
                                             vmem_limit_bytes=VMEM_LIMIT),
        name="rwkv_scan",
    )(*ins, s0)


def _attn_kernel(q_ref, k_ref, wuv_ref, o_ref, m_s, l_s, acc_s, *, tq, n_main, n_valid_extra):
    i = pl.program_id(1)
    rows = HEADS * tq
    q = q_ref[0].reshape(rows, QK_W)
    m_s[...] = jnp.full(m_s.shape, NEG, F32)
    l_s[...] = jnp.zeros(l_s.shape, F32)
    acc_s[...] = jnp.zeros(acc_s.shape, F32)

    def group(kb, fix):
        s = lax.dot_general(q, kb, (((1,), (1,)), ((), ())), preferred_element_type=F32)
        if fix is not None:
            s = fix(s)
        m_prev = m_s[...]
        m_next = jnp.maximum(m_prev, jnp.max(s, axis=1, keepdims=True))
        alpha = jnp.exp2(m_prev - m_next)
        pr = jnp.exp2(s - jnp.concatenate([m_next] * (kb.shape[0] // LANE), axis=1))
        l_s[...] = alpha * l_s[...] + jnp.sum(pr, axis=1, keepdims=True)
        acc_s[...] = alpha * acc_s[...] + jnp.dot(pr.astype(BF16), kb[:, :KV_LORA], preferred_element_type=F32)
        m_s[...] = m_next

    if n_main:
        def body(g, carry):
            group(k_ref[0, pl.ds(pl.multiple_of(g * ATT_GROUP, ATT_GROUP), ATT_GROUP), :], None)
            return carry

        lax.fori_loop(0, i // 2, body, 0)
        odd = (i % 2) == 1
        k_odd = k_ref[0, pl.ds(pl.multiple_of(jnp.maximum(i - 1, 0) * tq, tq), tq), :]
        k_diag = k_ref[0, pl.ds(pl.multiple_of(i * tq, tq), tq), :]
        k_extra = k_ref[0, n_main:n_main + LANE, :]
        bias_odd = jnp.where(odd, 0.0, NEG)
        rc = (lax.broadcasted_iota(jnp.int32, (rows, tq), 0) % tq) // CHUNK
        cc = lax.broadcasted_iota(jnp.int32, (rows, tq), 1) // CHUNK
        diag_ok = cc <= rc
        extra_ok = lax.broadcasted_iota(jnp.int32, (rows, LANE), 1) < n_valid_extra

        def fix(s):
            return jnp.concatenate([s[:, :tq] + bias_odd, jnp.where(diag_ok, s[:, tq:2 * tq], NEG),
                                    jnp.where(extra_ok, s[:, 2 * tq:], NEG)], axis=1)

        group(jnp.concatenate([k_odd, k_diag, k_extra], axis=0), fix)
    else:
        n_groups = k_ref.shape[1] // ATT_GROUP
        for g in range(n_groups):
            valid = min(ATT_GROUP, n_valid_extra - g * ATT_GROUP)
            fix = None
            if valid < ATT_GROUP:
                ok = lax.broadcasted_iota(jnp.int32, (rows, ATT_GROUP), 1) < valid
                fix = lambda s, ok=ok: jnp.where(ok, s, NEG)
            group(k_ref[0, g * ATT_GROUP:(g + 1) * ATT_GROUP, :], fix)

    lat = acc_s[...] / l_s[...]
    lat_wide = jnp.concatenate([lat[hh * tq:(hh + 1) * tq] for hh in range(HEADS)], axis=1).astype(BF16)
    o_ref[0] = jnp.dot(lat_wide, wuv_ref[...], preferred_element_type=F32)


def _attention(q, keys, wuv, tq, n_main, n_valid_extra):
    B, _, T, _ = q.shape
    tkeys = keys.shape[1]
    kern = functools.partial(_attn_kernel, tq=tq, n_main=n_main, n_valid_extra=n_valid_extra)
    rows = HEADS * tq
    return pl.pallas_call(
        kern,
        grid=(B, T // tq),
        in_specs=[pl.BlockSpec((1, HEADS, tq, QK_W), lambda b, i: (b, 0, i, 0)),
                  pl.BlockSpec((1, tkeys, QK_W), lambda b, i: (b, 0, 0)),
                  _full(wuv.shape)],
        out_specs=pl.BlockSpec((1, tq, RW), lambda b, i: (b, i, 0)),
        out_shape=jax.ShapeDtypeStruct((B, T, RW), F32),
        scratch_shapes=[pltpu.VMEM((rows, LANE), F32)] * 3,
        compiler_params=pltpu.CompilerParams(dimension_semantics=("arbitrary", "arbitrary"),
                                             vmem_limit_bytes=VMEM_LIMIT),
        name="mla_attention",
    )(q, keys, wuv)


def _stage_d_kernel(x1_ref, y_ref, bon_ref, g_ref, mla_ref, ones_ref, lnw_ref, lnb_ref, wo_ref,
                    n2_ref, w1_ref, w3_ref, w2_ref, fn_ref, o_ref):
    y = y_ref[...]
    mu = _dot_x2(y, ones_ref[...]) * (1.0 / HD)
    d = y - mu
    var = _dot_x2(d * d, ones_ref[...]) * (1.0 / HD)
    yn = d * lax.rsqrt(var + GN_EPS) * lnw_ref[...] + lnb_ref[...]
    rw = ((yn + bon_ref[...]) * g_ref[...]).astype(BF16)
    x2 = (x1_ref[...] + jnp.dot(rw, wo_ref[:RW, :], preferred_element_type=F32)
          + jnp.dot(mla_ref[...].astype(BF16), wo_ref[RW:, :], preferred_element_type=F32))
    h = _rms(x2, n2_ref[...]).astype(BF16)
    x3 = x2 + 0.5 * _ffn(h, w1_ref, w3_ref, w2_ref)
    o_ref[...] = _rms(x3, fn_ref[...])


def _stage_d(x1, y, bon, g, mla, W, tm):
    B, T, _ = x1.shape
    n = B * T
    flat = lambda t: t.reshape(n, t.shape[-1])
    tok = lambda w: pl.BlockSpec((tm, w), lambda i: (i, 0))
    weights = [W["ones"], W["lnw"], W["lnb"], W["wout"], W["n2"], W["f2w1"], W["f2w3"], W["f2w2"], W["fnorm"]]
    out = pl.pallas_call(
        _stage_d_kernel,
        grid=(n // tm,),
        in_specs=[tok(D_MODEL), tok(RW), tok(RW), tok(RW), tok(RW)] + [_full(w.shape) for w in weights],
        out_specs=tok(D_MODEL),
        out_shape=jax.ShapeDtypeStruct((n, D_MODEL), F32),
        compiler_params=pltpu.CompilerParams(dimension_semantics=("arbitrary",), vmem_limit_bytes=VMEM_LIMIT),
        name="stage_d",
    )(flat(x1), flat(y), flat(bon), flat(g), flat(mla), *weights)
    return out.reshape(B, T, D_MODEL)


def _pad_rw_cols(t):
    z = jnp.zeros(t.shape[:-1] + (HD,), t.dtype)
    return jnp.concatenate([t[..., :1536], t[..., 1536:1600], z, t[..., 1600:1664], z, t[..., 1664:1792]], axis=-1)


def _unpad_rw_cols(t):
    return jnp.concatenate([t[..., :1536], t[..., O_WL:O_WL + 64], t[..., O_AL:O_AL + 64], t[..., O_GL:O_GL + LANE]],
                           axis=-1)


def _rot_cols(w):
    half = ROPE // 2
    return jnp.concatenate([-w[..., half:], w[..., :half]], axis=-1)


def _prep_weights(norm_ffn1, ffn1_w1, ffn1_w3, ffn1_w2, norm_mix, w_in, mu_shift, w0, w_w2, a0, w_a2, w_g2, k_k,
                  k_a, r_k, ln_x_w, ln_x_b, q_norm, w_q_up, kv_norm, w_uk, w_uv, w_out, norm_ffn2, ffn2_w1,
                  ffn2_w3, ffn2_w2, final_norm):
    row = lambda t: t.reshape(1, -1).astype(F32)
    zc = lambda n: jnp.zeros((D_MODEL, n), F32)
    wi = w_in[0]
    wkr = wi[:, 2176:2208]
    win = jnp.concatenate([_pad_rw_cols(wi[:, :1792]), wi[:, 1792:2176], wkr, zc(LANE - ROPE), _rot_cols(wkr),
                           zc(LANE - ROPE)], axis=1)
    zl = jnp.zeros((HD, RW), F32)
    wq = w_q_up[0]
    wq_rope = wq[:, :, HD:]
    pad_heads = lambda t: jnp.pad(t, ((0, 0), (0, 0), (0, LANE - ROPE))).reshape(Q_LORA, HEADS * LANE)
    eye = jnp.eye(HEADS, dtype=F32)
    wuk_bd = jnp.einsum("chd,hg->hdgc", w_uk[0], eye).reshape(HEADS * HD, HEADS * KV_LORA)
    wuv_bd = jnp.einsum("chd,hg->hcgd", w_uv[0], eye).reshape(HEADS * KV_LORA, RW)
    ones_bd = jnp.kron(eye, jnp.ones((HD, HD), F32))
    return dict(
        n1=row(norm_ffn1), f1w1=ffn1_w1[0].astype(BF16), f1w3=ffn1_w3[0].astype(BF16), f1w2=ffn1_w2[0].astype(BF16),
        nmix=row(norm_mix), win=win.astype(BF16), mu=_pad_rw_cols(row(mu_shift)),
        w0=row(w0), ww2=jnp.concatenate([w_w2[0], zl], axis=0).astype(BF16), a0=row(a0),
        wa2=jnp.concatenate([w_a2[0], zl], axis=0).astype(BF16), wg2=w_g2[0].astype(BF16),
        kk=row(k_k), ka=row(k_a), rk=row(r_k), ones=ones_bd.astype(BF16),
        qn=row(q_norm), wqn=wq[:, :, :HD].reshape(Q_LORA, RW).astype(BF16),
        wqr=pad_heads(wq_rope).astype(BF16), wqrr=pad_heads(_rot_cols(wq_rope)).astype(BF16),
        kvn=row(kv_norm), wuk=wuk_bd.astype(BF16), wuv=wuv_bd.astype(BF16),
        lnw=row(ln_x_w), lnb=row(ln_x_b), wout=w_out[0].astype(BF16),
        n2=row(norm_ffn2), f2w1=ffn2_w1[0].astype(BF16), f2w3=ffn2_w3[0].astype(BF16),
        f2w2=ffn2_w2[0].astype(BF16), fnorm=row(final_norm))


def _rope_tables(pos):
    half = ROPE // 2
    inv = ROPE_THETA ** (-jnp.arange(half, dtype=F32) / half)
    ang = pos.astype(F32)[:, None] * inv[None, :]
    pad = lambda t: jnp.pad(jnp.concatenate([t, t], axis=1), ((0, 0), (0, LANE - ROPE)))
    return pad(jnp.cos(ang)), pad(jnp.sin(ang))


def _state_to_pairs(s):
    B = s.shape[0]
    s = s.reshape(B, N_PAIR, 2, HD, HD)
    z = jnp.zeros_like(s[:, :, 0])
    top = jnp.concatenate([s[:, :, 0], z], axis=-1)
    bot = jnp.concatenate([z, s[:, :, 1]], axis=-1)
    return jnp.concatenate([top, bot], axis=-2)


def _pairs_to_state(s):
    B = s.shape[0]
    return jnp.stack([s[:, :, :HD, :HD], s[:, :, HD:, HD:]], axis=2).reshape(B, HEADS, HD, HD)


def _pad_rows(t, n):
    return jnp.pad(t, ((0, 0), (0, n - t.shape[1]), (0, 0)))


def _tile(n, pref):
    return pref if n % pref == 0 else n


def kernel(x_prompt, x_sample, cache_ckv, cache_krope, state_wkv, state_shift, meta_tokens, norm_ffn1, ffn1_w1, ffn1_w3, ffn1_w2, norm_mix, w_in, mu_shift, w0, w_w2, a0, w_a2, w_g2, k_k, k_a, r_k, ln_x_w, ln_x_b, q_norm, w_q_up, kv_norm, w_uk, w_uv, w_out, norm_ffn2, ffn2_w1, ffn2_w3, ffn2_w2, final_norm):
    assert w_in.shape[0] == 1, "single-layer stack"
    B, S, _ = x_prompt.shape
    Bd, Tn, _ = x_sample.shape
    past = cache_ckv.shape[2]
    assert S % ATT_GROUP == 0 and Tn % SCAN == 0
    W = _prep_weights(norm_ffn1, ffn1_w1, ffn1_w3, ffn1_w2, norm_mix, w_in, mu_shift, w0, w_w2, a0, w_a2, w_g2,
                      k_k, k_a, r_k, ln_x_w, ln_x_b, q_norm, w_q_up, kv_norm, w_uk, w_uv, w_out, norm_ffn2,
                      ffn2_w1, ffn2_w3, ffn2_w2, final_norm)
    scan_keys = ("r", "lw", "k", "v", "a", "b")

    cos_m, sin_m = _rope_tables(jnp.arange(N_META))
    am = _stage_a(meta_tokens[None].astype(F32), jnp.zeros((1, 1, RWP), F32), cos_m, sin_m, W, N_META)
    _, st_m = _rwkv([_pad_rows(am[n], SCAN) for n in scan_keys], jnp.zeros((1, N_PAIR, PAIR, PAIR), F32))

    tm = _tile(S, 256)
    cos_p, sin_p = _rope_tables(N_META + jnp.arange(S))
    ap = _stage_a(x_prompt, jnp.broadcast_to(am["shift"], (B, 1, RWP)), cos_p, sin_p, W, tm)
    y_p, st_p = _rwkv([ap[n] for n in scan_keys], jnp.broadcast_to(st_m, (B, N_PAIR, PAIR, PAIR)))
    keys_p = jnp.concatenate([ap["kcat"], jnp.broadcast_to(am["kcat"], (B, N_META, QK_W)),
                              jnp.zeros((B, LANE - N_META, QK_W), BF16)], axis=1)
    mla_p = _attention(ap["q"], keys_p, W["wuv"], ATT_GROUP // 2, S, N_META)
    y_prompt = _stage_d(ap["x1"], y_p, ap["bon"], ap["g"], mla_p, W, _tile(B * S, 512))

    cos_s, sin_s = _rope_tables(N_META + past + jnp.arange(Tn))
    as_ = _stage_a(x_sample, _pad_rw_cols(state_shift[0].astype(F32)), cos_s, sin_s, W, Tn)
    y_s, st_s = _rwkv([as_[n] for n in scan_keys], _state_to_pairs(state_wkv[0].astype(F32)))
    cache_k = jnp.concatenate([cache_ckv[0], cache_krope[0], jnp.zeros((Bd, past, QK_W - KV_LORA - ROPE), F32)],
                              axis=-1).astype(BF16)
    n_valid = past + Tn + N_META
    n_keys = -(-n_valid // ATT_GROUP) * ATT_GROUP
    keys_s = jnp.concatenate([cache_k, as_["kcat"], jnp.broadcast_to(am["kcat"], (Bd, N_META, QK_W)),
                              jnp.zeros((Bd, n_keys - n_valid, QK_W), BF16)], axis=1)
    mla_s = _attention(as_["q"], keys_s, W["wuv"], Tn, 0, n_valid)
    y_sample = _stage_d(as_["x1"], y_s, as_["bon"], as_["g"], mla_s, W, Tn)

    bc = lambda t, n: jnp.broadcast_to(t, (n,) + t.shape[1:])
    ckv_p = jnp.concatenate([bc(am["c"], B), ap["c"]], axis=1)[None]
    kr_p = jnp.concatenate([bc(am["kr"], B), ap["kr"]], axis=1)[None, ..., :ROPE]
    return (y_prompt, y_sample, ckv_p, kr_p, _pairs_to_state(st_p)[None], _unpad_rw_cols(ap["shift"])[None],
            as_["c"][None], as_["kr"][None, ..., :ROPE], _pairs_to_state(st_s)[None], _unpad_rw_cols(as_["shift"])[None])
```

```python
import functools

import jax
import jax.numpy as jnp
from jax import lax
from jax.experimental import pallas as pl
from jax.experimental.pallas import tpu as pltpu

F32 = jnp.float32
BF16 = jnp.bfloat16

D_MODEL = 1024
D_FF = 2816
CHUNK = 64
N_META = 16
HEADS = 8
HD = 64
RW = HEADS * HD
ROPE = 32
Q_LORA = 256
KV_LORA = 128
NORM_EPS = 1e-6
GN_EPS = 64e-5
MLA_SCALE = (HD + ROPE) ** -0.5
ROPE_THETA = 10000.0

LANE = 128
SCAN = 64
PAIR = 2 * HD
N_PAIR = HEADS // 2
RWKV_ROWS = 4
QK_W = 256

O_R, O_K, O_V, O_WA, O_GL = 0, 512, 1024, 1536, 1664
RWC = 1792
O_Q, O_KV, O_KR = 1792, 2048, 2176
PW = 2304
ROT_SHIFT = LANE - ROPE
FF_CHUNK = 1408
VMEM_LIMIT = 56 * 1024 * 1024
NEG = -1e30
LOG2E = 1.4426950408889634
ATT_GROUP = 512


def _dot(a, b):
    return jnp.dot(a.astype(BF16), b.astype(BF16), preferred_element_type=F32)


def _dot_nt(a, b):
    return lax.dot_general(a.astype(BF16), b.astype(BF16), (((1,), (1,)), ((), ())),
                           preferred_element_type=F32)


def _split2(x):
    hi = x.astype(BF16)
    lo = (x - hi.astype(F32)).astype(BF16)
    return hi, lo


def _dot_x2(a, b):
    hi, lo = _split2(a)
    bb = b.astype(BF16)
    return (jnp.dot(hi, bb, preferred_element_type=F32) + jnp.dot(lo, bb, preferred_element_type=F32))


def _rms(x, g):
    return x * lax.rsqrt(jnp.mean(x * x, axis=-1, keepdims=True) + NORM_EPS) * g


def _ffn(h, w1_ref, w3_ref, w2_ref):
    acc = None
    for c in range(D_FF // FF_CHUNK):
        sl = slice(c * FF_CHUNK, (c + 1) * FF_CHUNK)
        a = jnp.dot(h, w1_ref[:, sl], preferred_element_type=F32)
        b = jnp.dot(h, w3_ref[:, sl], preferred_element_type=F32)
        z = (a * jax.nn.sigmoid(a) * b).astype(BF16)
        t = jnp.dot(z, w2_ref[sl, :], preferred_element_type=F32)
        acc = t if acc is None else acc + t
    return acc


def _ffn1_kernel(x_ref, n1_ref, w1_ref, w3_ref, w2_ref, o_ref):
    x = x_ref[...]
    h = _rms(x, n1_ref[...]).astype(BF16)
    o_ref[...] = x + 0.5 * _ffn(h, w1_ref, w3_ref, w2_ref)


def _full(shape):
    return pl.BlockSpec(memory_space=pltpu.VMEM)


def _ffn1(x, W, tm):
    B, T, _ = x.shape
    n = B * T
    tok = pl.BlockSpec((tm, D_MODEL), lambda i: (i, 0))
    weights = [W["n1"], W["f1w1"], W["f1w3"], W["f1w2"]]
    out = pl.pallas_call(
        _ffn1_kernel,
        grid=(n // tm,),
        in_specs=[tok] + [_full(w.shape) for w in weights],
        out_specs=tok,
        out_shape=jax.ShapeDtypeStruct((n, D_MODEL), F32),
        compiler_params=pltpu.CompilerParams(dimension_semantics=("arbitrary",), vmem_limit_bytes=VMEM_LIMIT),
        name="ffn1",
    )(x.reshape(n, D_MODEL), *weights)
    return out.reshape(B, T, D_MODEL)


def _rope(blk, cos, sin):
    return blk * cos + pltpu.roll(blk, ROT_SHIFT, 1) * sin


def _stage_a_kernel(x1_ref, sh0_ref, cos_ref, sin_ref,
                    nmix_ref, win_ref, mu_ref,
                    w0_ref, ww2_ref, a0_ref, wa2_ref, wg2_ref, kk_ref, ka_ref, rk_ref, ones_ref,
                    qn_ref, wqn_ref, wqr_ref, kvn_ref, wuk_ref,
                    r_o, lw_o, k_o, v_o, a_o, b_o, bon_o, g_o, c_o, kr_o, kcat_o, q_o, sh_o,
                    prev_s, *, tm):
    t = pl.program_id(1)
    hm = _rms(x1_ref[0], nmix_ref[...]).astype(BF16)
    p = jnp.dot(hm, win_ref[...], preferred_element_type=F32)

    prw = p[:, :RWC]

    @pl.when(t == 0)
    def _():
        prev_s[...] = sh0_ref[0]

    prev_row = prev_s[...]
    rolled = pltpu.roll(prw, 1, 0)
    row = lax.broadcasted_iota(jnp.int32, prw.shape, 0)
    prev = jnp.where(row == 0, prev_row, rolled)
    last = prw[tm - 1:tm, :]
    prev_s[...] = last
    sh_o[0] = last
    s = prw + mu_ref[...] * (prev - prw)

    r = s[:, O_R:O_R + RW]
    k = s[:, O_K:O_K + RW]
    v = s[:, O_V:O_V + RW]
    wa = s[:, O_WA:O_WA + LANE]
    gl = s[:, O_GL:O_GL + LANE]
    z = w0_ref[...] + _dot(jnp.tanh(wa), ww2_ref[...])
    softplus = jnp.maximum(-z, 0.0) + jnp.log(1.0 + jnp.exp(-jnp.abs(z)))
    logw = -softplus - 0.5
    lw_o[0] = -jnp.exp(logw)
    a_sig = jax.nn.sigmoid(a0_ref[...] + _dot(wa, wa2_ref[...]))
    g_o[0] = _dot(jax.nn.sigmoid(gl), wg2_ref[...])
    kk = k * kk_ref[...]
    ss = _dot(kk * kk, ones_ref[...])
    kk = kk / jnp.maximum(jnp.sqrt(ss), 1e-12)
    k2 = k * (1.0 + (a_sig - 1.0) * ka_ref[...])
    r_o[0] = r
    k_o[0] = k2
    v_o[0] = v.astype(BF16)
    a_o[0] = -kk
    b_o[0] = kk * a_sig
    bon_o[0] = _dot(r * k2 * rk_ref[...], ones_ref[...]) * v

    cos = cos_ref[...]
    sin = sin_ref[...]
    c = _rms(p[:, O_KV:O_KV + KV_LORA], kvn_ref[...])
    kr = _rope(p[:, O_KR:O_KR + LANE], cos, sin)
    c_o[0] = c
    kr_o[0] = kr
    kcat_o[0] = jnp.concatenate([c, kr], axis=1).astype(BF16)
    cq = _rms(p[:, O_Q:O_Q + Q_LORA], qn_ref[...]).astype(BF16)
    qn = jnp.dot(cq, wqn_ref[...], preferred_element_type=F32)
    qlat = _dot(qn, wuk_ref[...])
    qr = jnp.dot(cq, wqr_ref[...], preferred_element_type=F32)
    for hh in range(HEADS):
        sl = slice(hh * LANE, (hh + 1) * LANE)
        qrh = _rope(qr[:, sl], cos, sin)
        q_o[0, hh] = (jnp.concatenate([qlat[:, sl], qrh], axis=1) * (MLA_SCALE * LOG2E)).astype(BF16)


def _stage_a(x1, sh0, cos, sin, W, tm):
    B, T, _ = x1.shape
    nt = T // tm
    tok = lambda w: pl.BlockSpec((1, tm, w), lambda b, t: (b, t, 0))
    weights = [W["nmix"], W["win"], W["mu"],
               W["w0"], W["ww2"], W["a0"], W["wa2"], W["wg2"], W["kk"], W["ka"], W["rk"], W["ones"],
               W["qn"], W["wqn"], W["wqr"], W["kvn"], W["wuk"]]
    in_specs = ([tok(D_MODEL), pl.BlockSpec((1, 1, RWC), lambda b, t: (b, 0, 0)),
                 pl.BlockSpec((tm, LANE), lambda b, t: (t, 0)), pl.BlockSpec((tm, LANE), lambda b, t: (t, 0))]
                + [_full(w.shape) for w in weights])
    f = lambda w, dt=F32: jax.ShapeDtypeStruct((B, T, w), dt)
    names = ["r", "lw", "k", "v", "a", "b", "bon", "g", "c", "kr", "kcat", "q", "shift"]
    out_shape = [f(RW), f(RW), f(RW), f(RW, BF16), f(RW), f(RW), f(RW), f(RW), f(KV_LORA), f(LANE), f(QK_W, BF16),
                 jax.ShapeDtypeStruct((B, HEADS, T, QK_W), BF16), jax.ShapeDtypeStruct((B, 1, RWC), F32)]
    out_specs = ([tok(RW)] * 8 + [tok(KV_LORA), tok(LANE), tok(QK_W),
                                  pl.BlockSpec((1, HEADS, tm, QK_W), lambda b, t: (b, 0, t, 0)),
                                  pl.BlockSpec((1, 1, RWC), lambda b, t: (b, 0, 0))])
    outs = pl.pallas_call(
        functools.partial(_stage_a_kernel, tm=tm),
        grid=(B, nt), in_specs=in_specs, out_specs=out_specs, out_shape=out_shape,
        scratch_shapes=[pltpu.VMEM((1, RWC), F32)],
        compiler_params=pltpu.CompilerParams(dimension_semantics=("arbitrary", "arbitrary"),
                                             vmem_limit_bytes=VMEM_LIMIT),
        name="stage_a",
    )(x1, sh0, cos, sin, *weights)
    return dict(zip(names, outs))


def _stack(x, lo_mask):
    return jnp.concatenate([jnp.where(lo_mask, x, 0.0), jnp.where(lo_mask, 0.0, x)], axis=0)


def _rwkv_kernel(r_ref, lw_ref, k_ref, v_ref, a_ref, b_ref, s0_ref, y_o, st_o, st_s, *, nb):
    c = pl.program_id(1)

    @pl.when(c == 0)
    def _():
        st_s[...] = s0_ref[...]

    n2 = 2 * SCAN
    ti = lax.broadcasted_iota(jnp.int32, (SCAN, SCAN), 0)
    tj = lax.broadcasted_iota(jnp.int32, (SCAN, SCAN), 1)
    tri = jnp.where(ti >= tj, 1.0, 0.0).astype(BF16)
    ri = lax.broadcasted_iota(jnp.int32, (n2, n2), 0) % SCAN
    ci = lax.broadcasted_iota(jnp.int32, (n2, n2), 1) % SCAN
    strict = ri > ci
    incl = ri >= ci
    lo_mask = lax.broadcasted_iota(jnp.int32, (SCAN, PAIR), 1) < HD

    chains = [(i, j) for i in range(nb) for j in range(N_PAIR)]
    sl = lambda j: slice(j * PAIR, (j + 1) * PAIR)
    sts = [st_s[i, j] for i, j in chains]
    ins = [[ref[i, :, sl(j)] for ref in (r_ref, k_ref, v_ref, a_ref, b_ref)] for i, j in chains]

    cums, lws = [], []
    for i in range(nb):
        lw_all = lw_ref[i]
        l1 = lw_all.astype(BF16)
        d1 = lw_all - l1.astype(F32)
        l2 = d1.astype(BF16)
        l3 = (d1 - l2.astype(F32)).astype(BF16)
        cum_all = (jnp.dot(tri, l1, preferred_element_type=F32) + jnp.dot(tri, l2, preferred_element_type=F32)
                   + jnp.dot(tri, l3, preferred_element_type=F32))
        for j in range(N_PAIR):
            cums.append(cum_all[:, sl(j)])
            lws.append(lw_all[:, sl(j)])

    def prep(n):
        r, k, v, a, b = ins[n]
        cum = cums[n]
        cum_c = cum[SCAN - 1:SCAN, :]
        e_n = jnp.exp(-cum)
        e_c = jnp.exp(cum_c - cum)
        return dict(
            p_c=jnp.exp(cum_c),
            ls=_stack(a * jnp.exp(cum - lws[n]), lo_mask).astype(BF16),
            rs=_stack(r * jnp.exp(cum), lo_mask),
            bs=_stack(b * e_n, lo_mask).astype(BF16), ks=_stack(k * e_n, lo_mask).astype(BF16),
            vs=_stack(v.astype(F32), lo_mask).astype(BF16),
            bh=_stack(b * e_c, lo_mask).astype(BF16), kh=_stack(k * e_c, lo_mask).astype(BF16))

    P = [prep(n) for n in range(len(chains))]
    aas = [_dot_nt(jnp.concatenate([p["ls"], p["rs"].astype(BF16)], axis=0),
                   jnp.concatenate([p["bs"], p["ks"]], axis=0)) for p in P]
    pws = [jnp.where(strict, aa[:n2, :n2], 0.0) for aa in aas]
    aks = [jnp.where(strict, aa[:n2, n2:], 0.0) for aa in aas]
    rbs = [jnp.where(incl, aa[n2:, :n2], 0.0) for aa in aas]
    rks = [jnp.where(incl, aa[n2:, n2:], 0.0) for aa in aas]

    xs = [jnp.concatenate([p["ls"].astype(F32), _dot(ak, p["vs"])], axis=1) for p, ak in zip(P, aks)]
    for it in range(6):
        xs = [x + _dot(pw, x) for x, pw in zip(xs, pws)]
        if it < 5:
            pws = [_dot(pw, pw) for pw in pws]

    ys_out, st_new = [], []
    for n, p in enumerate(P):
        wl = xs[n][:, :PAIR]
        uv = jnp.concatenate([xs[n][:, PAIR:], p["vs"].astype(F32)], axis=0)
        rq = p["rs"] + _dot(rbs[n], wl)
        y0 = _dot(jnp.concatenate([rbs[n], rks[n]], axis=1), uv)
        g_m = _dot(wl.T, p["bh"])
        h_m = _dot(uv.T, jnp.concatenate([p["bh"], p["kh"]], axis=0))
        ys = _dot_nt(rq, sts[n]) + y0
        ys_out.append(ys[:SCAN] + ys[SCAN:])
        st_new.append(p["p_c"] * sts[n] + _dot_x2(sts[n], g_m) + h_m)

    for i in range(nb):
        y_o[i] = jnp.concatenate(ys_out[i * N_PAIR:(i + 1) * N_PAIR], axis=1)
    for n, (i, j) in enumerate(chains):
        st_s[i, j] = st_new[n]
        st_o[i, j] = st_new[n]


def _rwkv(ins, s0):
    B, T, _ = ins[0].shape
    nc = T // SCAN
    nb = RWKV_ROWS if B % RWKV_ROWS == 0 else 1
    tok = pl.BlockSpec((nb, SCAN, RW), lambda b, c: (b, c, 0))
    st_spec = pl.BlockSpec((nb, N_PAIR, PAIR, PAIR), lambda b, c: (b, 0, 0, 0))
    return pl.pallas_call(
        functools.partial(_rwkv_kernel, nb=nb),
        grid=(B // nb, nc), in_specs=[tok] * 6 + [st_spec], out_specs=[tok, st_spec],
        out_shape=[jax.ShapeDtypeStruct((B, T, RW), F32), jax.ShapeDtypeStruct((B, N_PAIR, PAIR, PAIR), F32)],
        scratch_shapes=[pltpu.VMEM((nb, N_PAIR, PAIR, PAIR), F32)],
        compiler_params=pltpu.CompilerParams(dimension_semantics=("arbitrary", "arbitrary"),
                                             vmem_limit_bytes=VMEM_LIMIT),
        name="rwkv_scan",
    )(*ins, s0)


def _attn_kernel(q_ref, k_ref, wuv_ref, o_ref, m_s, l_s, acc_s, *, tq, n_main, n_valid_extra):
    i = pl.program_id(1)
    rows = HEADS * tq
    q = q_ref[0].reshape(rows, QK_W)
    m_s[...] = jnp.full(m_s.shape, NEG, F32)
    l_s[...] = jnp.zeros(l_s.shape, F32)
    acc_s[...] = jnp.zeros(acc_s.shape, F32)

    def group(kb, fix):
        s = lax.dot_general(q, kb, (((1,), (1,)), ((), ())), preferred_element_type=F32)
        if fix is not None:
            s = fix(s)
        m_prev = m_s[...]
        m_next = jnp.maximum(m_prev, jnp.max(s, axis=1, keepdims=True))
        alpha = jnp.exp2(m_prev - m_next)
        pr = jnp.exp2(s - jnp.concatenate([m_next] * (kb.shape[0] // LANE), axis=1))
        l_s[...] = alpha * l_s[...] + jnp.sum(pr, axis=1, keepdims=True)
        acc_s[...] = alpha * acc_s[...] + jnp.dot(pr.astype(BF16), kb[:, :KV_LORA], preferred_element_type=F32)
        m_s[...] = m_next

    if n_main:
        def body(g, carry):
            group(k_ref[0, pl.ds(pl.multiple_of(g * tq, tq), tq), :], None)
            return carry

        lax.fori_loop(0, i, body, 0)
        k_diag = k_ref[0, pl.ds(pl.multiple_of(i * tq, tq), tq), :]
        k_extra = k_ref[0, n_main:n_main + LANE, :]
        rc = (lax.broadcasted_iota(jnp.int32, (rows, tq), 0) % tq) // CHUNK
        cc = lax.broadcasted_iota(jnp.int32, (rows, tq), 1) // CHUNK
        diag_ok = cc <= rc
        extra_ok = lax.broadcasted_iota(jnp.int32, (rows, LANE), 1) < n_valid_extra

        def fix(s):
            return jnp.concatenate([jnp.where(diag_ok, s[:, :tq], NEG), jnp.where(extra_ok, s[:, tq:], NEG)], axis=1)

        group(jnp.concatenate([k_diag, k_extra], axis=0), fix)
    else:
        n_groups = k_ref.shape[1] // ATT_GROUP
        for g in range(n_groups):
            valid = min(ATT_GROUP, n_valid_extra - g * ATT_GROUP)
            fix = None
            if valid < ATT_GROUP:
                ok = lax.broadcasted_iota(jnp.int32, (rows, ATT_GROUP), 1) < valid
                fix = lambda s, ok=ok: jnp.where(ok, s, NEG)
            group(k_ref[0, g * ATT_GROUP:(g + 1) * ATT_GROUP, :], fix)

    lat = acc_s[...] / l_s[...]
    lat_wide = jnp.concatenate([lat[hh * tq:(hh + 1) * tq] for hh in range(HEADS)], axis=1).astype(BF16)
    o_ref[0] = jnp.dot(lat_wide, wuv_ref[...], preferred_element_type=F32)


def _attention(q, keys, wuv, tq, n_main, n_valid_extra):
    B, _, T, _ = q.shape
    tkeys = keys.shape[1]
    kern = functools.partial(_attn_kernel, tq=tq, n_main=n_main, n_valid_extra=n_valid_extra)
    rows = HEADS * tq
    return pl.pallas_call(
        kern,
        grid=(B, T // tq),
        in_specs=[pl.BlockSpec((1, HEADS, tq, QK_W), lambda b, i: (b, 0, i, 0)),
                  pl.BlockSpec((1, tkeys, QK_W), lambda b, i: (b, 0, 0)),
                  _full(wuv.shape)],
        out_specs=pl.BlockSpec((1, tq, RW), lambda b, i: (b, i, 0)),
        out_shape=jax.ShapeDtypeStruct((B, T, RW), F32),
        scratch_shapes=[pltpu.VMEM((rows, LANE), F32)] * 3,
        compiler_params=pltpu.CompilerParams(dimension_semantics=("arbitrary", "arbitrary"),
                                             vmem_limit_bytes=VMEM_LIMIT),
        name="mla_attention",
    )(q, keys, wuv)


def _stage_d_kernel(x1_ref, y_ref, bon_ref, g_ref, mla_ref, ones_ref, lnw_ref, lnb_ref, wo_ref,
                    n2_ref, w1_ref, w3_ref, w2_ref, fn_ref, o_ref):
    y = y_ref[...]
    mu = _dot_x2(y, ones_ref[...]) * (1.0 / HD)
    d = y - mu
    var = _dot(d * d, ones_ref[...]) * (1.0 / HD)
    yn = d * lax.rsqrt(var + GN_EPS) * lnw_ref[...] + lnb_ref[...]
    rw = ((yn + bon_ref[...]) * g_ref[...]).astype(BF16)
    x2 = (x1_ref[...] + jnp.dot(rw, wo_ref[:RW, :], preferred_element_type=F32)
          + jnp.dot(mla_ref[...].astype(BF16), wo_ref[RW:, :], preferred_element_type=F32))
    h = _rms(x2, n2_ref[...]).astype(BF16)
    x3 = x2 + 0.5 * _ffn(h, w1_ref, w3_ref, w2_ref)
    o_ref[...] = _rms(x3, fn_ref[...])


def _stage_d(x1, y, bon, g, mla, W, tm):
    B, T, _ = x1.shape
    n = B * T
    flat = lambda t: t.reshape(n, t.shape[-1])
    tok = lambda w: pl.BlockSpec((tm, w), lambda i: (i, 0))
    weights = [W["ones"], W["lnw"], W["lnb"], W["wout"], W["n2"], W["f2w1"], W["f2w3"], W["f2w2"], W["fnorm"]]
    out = pl.pallas_call(
        _stage_d_kernel,
        grid=(n // tm,),
        in_specs=[tok(D_MODEL), tok(RW), tok(RW), tok(RW), tok(RW)] + [_full(w.shape) for w in weights],
        out_specs=tok(D_MODEL),
        out_shape=jax.ShapeDtypeStruct((n, D_MODEL), F32),
        compiler_params=pltpu.CompilerParams(dimension_semantics=("arbitrary",), vmem_limit_bytes=VMEM_LIMIT),
        name="stage_d",
    )(flat(x1), flat(y), flat(bon), flat(g), flat(mla), *weights)
    return out.reshape(B, T, D_MODEL)


def _rot_cols(w):
    half = ROPE // 2
    return jnp.concatenate([-w[..., half:], w[..., :half]], axis=-1)


def _prep_weights(norm_ffn1, ffn1_w1, ffn1_w3, ffn1_w2, norm_mix, w_in, mu_shift, w0, w_w2, a0, w_a2, w_g2, k_k,
                  k_a, r_k, ln_x_w, ln_x_b, q_norm, w_q_up, kv_norm, w_uk, w_uv, w_out, norm_ffn2, ffn2_w1,
                  ffn2_w3, ffn2_w2, final_norm):
    row = lambda t: t.reshape(1, -1).astype(F32)
    wi = w_in[0]
    wkr = wi[:, O_KR:O_KR + ROPE]
    win = jnp.concatenate([wi, _rot_cols(wkr), jnp.zeros((D_MODEL, LANE - 2 * ROPE), F32)], axis=1)
    zl = jnp.zeros((HD, RW), F32)
    wq = w_q_up[0]
    wq_rope = wq[:, :, HD:]
    wqr = jnp.concatenate([wq_rope, _rot_cols(wq_rope), jnp.zeros((Q_LORA, HEADS, LANE - 2 * ROPE), F32)], axis=-1)
    eye = jnp.eye(HEADS, dtype=F32)
    wuk_bd = jnp.einsum("chd,hg->hdgc", w_uk[0], eye).reshape(HEADS * HD, HEADS * KV_LORA)
    wuv_bd = jnp.einsum("chd,hg->hcgd", w_uv[0], eye).reshape(HEADS * KV_LORA, RW)
    ones_bd = jnp.kron(eye, jnp.ones((HD, HD), F32))
    return dict(
        n1=row(norm_ffn1), f1w1=ffn1_w1[0].astype(BF16), f1w3=ffn1_w3[0].astype(BF16), f1w2=ffn1_w2[0].astype(BF16),
        nmix=row(norm_mix), win=win.astype(BF16), mu=row(mu_shift),
        w0=row(w0), ww2=jnp.concatenate([w_w2[0], zl], axis=0).astype(BF16), a0=row(a0),
        wa2=jnp.concatenate([zl, w_a2[0]], axis=0).astype(BF16), wg2=w_g2[0].astype(BF16),
        kk=row(k_k), ka=row(k_a), rk=row(r_k), ones=ones_bd.astype(BF16),
        qn=row(q_norm), wqn=wq[:, :, :HD].reshape(Q_LORA, RW).astype(BF16),
        wqr=wqr.reshape(Q_LORA, HEADS * LANE).astype(BF16),
        kvn=row(kv_norm), wuk=wuk_bd.astype(BF16), wuv=wuv_bd.astype(BF16),
        lnw=row(ln_x_w), lnb=row(ln_x_b), wout=w_out[0].astype(BF16),
        n2=row(norm_ffn2), f2w1=ffn2_w1[0].astype(BF16), f2w3=ffn2_w3[0].astype(BF16),
        f2w2=ffn2_w2[0].astype(BF16), fnorm=row(final_norm))


def _rope_tables(pos):
    half = ROPE // 2
    inv = ROPE_THETA ** (-jnp.arange(half, dtype=F32) / half)
    ang = pos.astype(F32)[:, None] * inv[None, :]
    pad = lambda t: jnp.pad(jnp.concatenate([t, t], axis=1), ((0, 0), (0, LANE - ROPE)))
    return pad(jnp.cos(ang)), pad(jnp.sin(ang))


def _state_to_pairs(s):
    B = s.shape[0]
    s = s.reshape(B, N_PAIR, 2, HD, HD)
    z = jnp.zeros_like(s[:, :, 0])
    top = jnp.concatenate([s[:, :, 0], z], axis=-1)
    bot = jnp.concatenate([z, s[:, :, 1]], axis=-1)
    return jnp.concatenate([top, bot], axis=-2)


def _pairs_to_state(s):
    B = s.shape[0]
    return jnp.stack([s[:, :, :HD, :HD], s[:, :, HD:, HD:]], axis=2).reshape(B, HEADS, HD, HD)


def _pad_rows(t, n):
    return jnp.pad(t, ((0, 0), (0, n - t.shape[1]), (0, 0)))


def _tile(n, pref):
    return pref if n % pref == 0 else n


def kernel(x_prompt, x_sample, cache_ckv, cache_krope, state_wkv, state_shift, meta_tokens, norm_ffn1, ffn1_w1, ffn1_w3, ffn1_w2, norm_mix, w_in, mu_shift, w0, w_w2, a0, w_a2, w_g2, k_k, k_a, r_k, ln_x_w, ln_x_b, q_norm, w_q_up, kv_norm, w_uk, w_uv, w_out, norm_ffn2, ffn2_w1, ffn2_w3, ffn2_w2, final_norm):
    assert w_in.shape[0] == 1, "single-layer stack"
    B, S, _ = x_prompt.shape
    Bd, Tn, _ = x_sample.shape
    past = cache_ckv.shape[2]
    assert S % ATT_GROUP == 0 and Tn % SCAN == 0
    W = _prep_weights(norm_ffn1, ffn1_w1, ffn1_w3, ffn1_w2, norm_mix, w_in, mu_shift, w0, w_w2, a0, w_a2, w_g2,
                      k_k, k_a, r_k, ln_x_w, ln_x_b, q_norm, w_q_up, kv_norm, w_uk, w_uv, w_out, norm_ffn2,
                      ffn2_w1, ffn2_w3, ffn2_w2, final_norm)
    scan_keys = ("r", "lw", "k", "v", "a", "b")

    cos_m, sin_m = _rope_tables(jnp.arange(N_META))
    xm = meta_tokens[None].astype(F32)
    am = _stage_a(_ffn1(xm, W, N_META), jnp.zeros((1, 1, RWC), F32), cos_m, sin_m, W, N_META)
    _, st_m = _rwkv([_pad_rows(am[n], SCAN) for n in scan_keys], jnp.zeros((1, N_PAIR, PAIR, PAIR), F32))

    tm = _tile(S, 512)
    cos_p, sin_p = _rope_tables(N_META + jnp.arange(S))
    x1_p = _ffn1(x_prompt, W, _tile(B * S, 1024))
    ap = _stage_a(x1_p, jnp.broadcast_to(am["shift"], (B, 1, RWC)), cos_p, sin_p, W, tm)
    y_p, st_p = _rwkv([ap[n] for n in scan_keys], jnp.broadcast_to(st_m, (B, N_PAIR, PAIR, PAIR)))
    keys_p = jnp.concatenate([ap["kcat"], jnp.broadcast_to(am["kcat"], (B, N_META, QK_W)),
                              jnp.zeros((B, LANE - N_META, QK_W), BF16)], axis=1)
    mla_p = _attention(ap["q"], keys_p, W["wuv"], ATT_GROUP, S, N_META)
    y_prompt = _stage_d(x1_p, y_p, ap["bon"], ap["g"], mla_p, W, tm)

    cos_s, sin_s = _rope_tables(N_META + past + jnp.arange(Tn))
    x1_s = _ffn1(x_sample, W, _tile(Bd * Tn, 512))
    as_ = _stage_a(x1_s, state_shift[0].astype(F32), cos_s, sin_s, W, Tn)
    y_s, st_s = _rwkv([as_[n] for n in scan_keys], _state_to_pairs(state_wkv[0].astype(F32)))
    cache_k = jnp.concatenate([cache_ckv[0], cache_krope[0], jnp.zeros((Bd, past, QK_W - KV_LORA - ROPE), F32)],
                              axis=-1).astype(BF16)
    n_valid = past + Tn + N_META
    n_keys = -(-n_valid // ATT_GROUP) * ATT_GROUP
    keys_s = jnp.concatenate([cache_k, as_["kcat"], jnp.broadcast_to(am["kcat"], (Bd, N_META, QK_W)),
                              jnp.zeros((Bd, n_keys - n_valid, QK_W), BF16)], axis=1)
    mla_s = _attention(as_["q"], keys_s, W["wuv"], Tn, 0, n_valid)
    y_sample = _stage_d(x1_s, y_s, as_["bon"], as_["g"], mla_s, W, _tile(Bd * Tn, 512))

    bc = lambda t, n: jnp.broadcast_to(t, (n,) + t.shape[1:])
    ckv_p = jnp.concatenate([bc(am["c"], B), ap["c"]], axis=1)[None]
    kr_p = jnp.concatenate([bc(am["kr"], B), ap["kr"]], axis=1)[None, ..., :ROPE]
    return (y_prompt, y_sample, ckv_p, kr_p, _pairs_to_state(st_p)[None], ap["shift"][None],
            as_["c"][None], as_["kr"][None, ..., :ROPE], _pairs_to_state(st_s)[None], as_["shift"][None])
```

```python
import functools

import jax
import jax.numpy as jnp
from jax import lax
from jax.experimental import pallas as pl
from jax.experimental.pallas import tpu as pltpu

F32 = jnp.float32
BF16 = jnp.bfloat16

D_MODEL = 1024
D_FF = 2816
CHUNK = 64
N_META = 16
HEADS = 8
HD = 64
RW = HEADS * HD
ROPE = 32
Q_LORA = 256
KV_LORA = 128
NORM_EPS = 1e-6
GN_EPS = 64e-5
MLA_SCALE = (HD + ROPE) ** -0.5
ROPE_THETA = 10000.0

LANE = 128
SCAN = 64
PAIR = 2 * HD
N_PAIR = HEADS // 2
RWKV_ROWS = 4
QK_W = 256

O_R, O_K, O_V, O_WA, O_GL = 0, 512, 1024, 1536, 1664
RWC = 1792
O_Q, O_KV, O_KR = 1792, 2048, 2176
PW = 2304
ROT_SHIFT = LANE - ROPE
FF_CHUNK = 256
VMEM_LIMIT = 56 * 1024 * 1024
NEG = -1e30
LOG2E = 1.4426950408889634
ATT_GROUP = 512


def _dot(a, b):
    return jnp.dot(a.astype(BF16), b.astype(BF16), preferred_element_type=F32)


def _dot_nt(a, b):
    return lax.dot_general(a.astype(BF16), b.astype(BF16), (((1,), (1,)), ((), ())),
                           preferred_element_type=F32)


def _split2(x):
    hi = x.astype(BF16)
    lo = (x - hi.astype(F32)).astype(BF16)
    return hi, lo


def _dot_x2(a, b):
    hi, lo = _split2(a)
    bb = b.astype(BF16)
    return (jnp.dot(hi, bb, preferred_element_type=F32) + jnp.dot(lo, bb, preferred_element_type=F32))


def _rms(x, g):
    return x * lax.rsqrt(jnp.mean(x * x, axis=-1, keepdims=True) + NORM_EPS) * g


def _ffn(h, w1_ref, w3_ref, w2_ref):
    acc = None
    for c in range(D_FF // FF_CHUNK):
        sl = slice(c * FF_CHUNK, (c + 1) * FF_CHUNK)
        a = jnp.dot(h, w1_ref[:, sl], preferred_element_type=F32)
        b = jnp.dot(h, w3_ref[:, sl], preferred_element_type=F32)
        z = (a * jax.nn.sigmoid(a) * b).astype(BF16)
        t = jnp.dot(z, w2_ref[sl, :], preferred_element_type=F32)
        acc = t if acc is None else acc + t
    return acc


def _ffn1_kernel(x_ref, n1_ref, w1_ref, w3_ref, w2_ref, o_ref):
    x = x_ref[...]
    h = _rms(x, n1_ref[...]).astype(BF16)
    o_ref[...] = x + 0.5 * _ffn(h, w1_ref, w3_ref, w2_ref)


def _full(shape):
    return pl.BlockSpec(memory_space=pltpu.VMEM)


def _ffn1(x, W, tm):
    B, T, _ = x.shape
    n = B * T
    tok = pl.BlockSpec((tm, D_MODEL), lambda i: (i, 0))
    weights = [W["n1"], W["f1w1"], W["f1w3"], W["f1w2"]]
    out = pl.pallas_call(
        _ffn1_kernel,
        grid=(n // tm,),
        in_specs=[tok] + [_full(w.shape) for w in weights],
        out_specs=tok,
        out_shape=jax.ShapeDtypeStruct((n, D_MODEL), F32),
        compiler_params=pltpu.CompilerParams(dimension_semantics=("arbitrary",), vmem_limit_bytes=VMEM_LIMIT),
        name="ffn1",
    )(x.reshape(n, D_MODEL), *weights)
    return out.reshape(B, T, D_MODEL)


def _rope(blk, cos, sin):
    return blk * cos + pltpu.roll(blk, ROT_SHIFT, 1) * sin


def _stage_a_kernel(x1_ref, sh0_ref, cos_ref, sin_ref,
                    nmix_ref, win_ref, mu_ref,
                    w0_ref, ww2_ref, a0_ref, wa2_ref, wg2_ref, kk_ref, ka_ref, rk_ref, ones_ref,
                    qn_ref, wqn_ref, wqr_ref, kvn_ref, wuk_ref,
                    r_o, lw_o, k_o, v_o, a_o, b_o, bon_o, g_o, c_o, kr_o, kcat_o, q_o, sh_o,
                    prev_s, *, tm, n_sub):
    t = pl.program_id(1)

    @pl.when(t == 0)
    def _():
        prev_s[...] = sh0_ref[0]

    ts = tm // n_sub

    def rows_block(rows, prev_row):
        hm = _rms(x1_ref[0, rows, :], nmix_ref[...]).astype(BF16)
        p = jnp.dot(hm, win_ref[...], preferred_element_type=F32)

        prw = p[:, :RWC]
        rolled = pltpu.roll(prw, 1, 0)
        row = lax.broadcasted_iota(jnp.int32, prw.shape, 0)
        prev = jnp.where(row == 0, prev_row, rolled)
        s = prw + mu_ref[...] * (prev - prw)

        r = s[:, O_R:O_R + RW]
        k = s[:, O_K:O_K + RW]
        v = s[:, O_V:O_V + RW]
        wa = s[:, O_WA:O_WA + LANE]
        gl = s[:, O_GL:O_GL + LANE]
        z = w0_ref[...] + _dot(jnp.tanh(wa), ww2_ref[...])
        softplus = jnp.maximum(-z, 0.0) + jnp.log(1.0 + jnp.exp(-jnp.abs(z)))
        logw = -softplus - 0.5
        a_sig = jax.nn.sigmoid(a0_ref[...] + _dot(wa, wa2_ref[...]))
        kk = k * kk_ref[...]
        ss = _dot(kk * kk, ones_ref[...])
        kk = kk / jnp.maximum(jnp.sqrt(ss), 1e-12)
        k2 = k * (1.0 + (a_sig - 1.0) * ka_ref[...])

        cos = cos_ref[rows, :]
        sin = sin_ref[rows, :]
        c = _rms(p[:, O_KV:O_KV + KV_LORA], kvn_ref[...])
        kr = _rope(p[:, O_KR:O_KR + LANE], cos, sin)
        cq = _rms(p[:, O_Q:O_Q + Q_LORA], qn_ref[...]).astype(BF16)
        qn = jnp.dot(cq, wqn_ref[...], preferred_element_type=F32)
        qlat = _dot(qn, wuk_ref[...])
        qr = jnp.dot(cq, wqr_ref[...], preferred_element_type=F32)
        q = []
        for hh in range(HEADS):
            sl = slice(hh * LANE, (hh + 1) * LANE)
            qrh = _rope(qr[:, sl], cos, sin)
            q.append((jnp.concatenate([qlat[:, sl], qrh], axis=1) * (MLA_SCALE * LOG2E)).astype(BF16))
        outs = dict(r=r, lw=-jnp.exp(logw), k=k2, v=v.astype(BF16), a=-kk, b=kk * a_sig,
                    bon=_dot(r * k2 * rk_ref[...], ones_ref[...]) * v, g=_dot(jax.nn.sigmoid(gl), wg2_ref[...]),
                    c=c, kr=kr, kcat=jnp.concatenate([c, kr], axis=1).astype(BF16), q=q)
        return outs, prw[ts - 1:ts, :]

    prev_row = prev_s[...]
    done = []
    for i in range(n_sub):
        rows = slice(i * ts, (i + 1) * ts)
        outs, prev_row = rows_block(rows, prev_row)
        done.append((rows, outs))
    prev_s[...] = prev_row
    sh_o[0] = prev_row
    refs = dict(r=r_o, lw=lw_o, k=k_o, v=v_o, a=a_o, b=b_o, bon=bon_o, g=g_o, c=c_o, kr=kr_o, kcat=kcat_o)
    for rows, outs in done:
        for name, ref in refs.items():
            ref[0, rows, :] = outs[name]
        for hh in range(HEADS):
            q_o[0, hh, rows, :] = outs["q"][hh]


def _stage_a(x1, sh0, cos, sin, W, tm):
    B, T, _ = x1.shape
    nt = T // tm
    tok = lambda w: pl.BlockSpec((1, tm, w), lambda b, t: (b, t, 0))
    weights = [W["nmix"], W["win"], W["mu"],
               W["w0"], W["ww2"], W["a0"], W["wa2"], W["wg2"], W["kk"], W["ka"], W["rk"], W["ones"],
               W["qn"], W["wqn"], W["wqr"], W["kvn"], W["wuk"]]
    in_specs = ([tok(D_MODEL), pl.BlockSpec((1, 1, RWC), lambda b, t: (b, 0, 0)),
                 pl.BlockSpec((tm, LANE), lambda b, t: (t, 0)), pl.BlockSpec((tm, LANE), lambda b, t: (t, 0))]
                + [_full(w.shape) for w in weights])
    f = lambda w, dt=F32: jax.ShapeDtypeStruct((B, T, w), dt)
    names = ["r", "lw", "k", "v", "a", "b", "bon", "g", "c", "kr", "kcat", "q", "shift"]
    out_shape = [f(RW), f(RW), f(RW), f(RW, BF16), f(RW), f(RW), f(RW), f(RW), f(KV_LORA), f(LANE), f(QK_W, BF16),
                 jax.ShapeDtypeStruct((B, HEADS, T, QK_W), BF16), jax.ShapeDtypeStruct((B, 1, RWC), F32)]
    out_specs = ([tok(RW)] * 8 + [tok(KV_LORA), tok(LANE), tok(QK_W),
                                  pl.BlockSpec((1, HEADS, tm, QK_W), lambda b, t: (b, 0, t, 0)),
                                  pl.BlockSpec((1, 1, RWC), lambda b, t: (b, 0, 0))])
    outs = pl.pallas_call(
        functools.partial(_stage_a_kernel, tm=tm, n_sub=2 if tm % 512 == 0 else 1),
        grid=(B, nt), in_specs=in_specs, out_specs=out_specs, out_shape=out_shape,
        scratch_shapes=[pltpu.VMEM((1, RWC), F32)],
        compiler_params=pltpu.CompilerParams(dimension_semantics=("arbitrary", "arbitrary"),
                                             vmem_limit_bytes=VMEM_LIMIT),
        name="stage_a",
    )(x1, sh0, cos, sin, *weights)
    return dict(zip(names, outs))


def _stack(x, lo_mask):
    return jnp.concatenate([jnp.where(lo_mask, x, 0.0), jnp.where(lo_mask, 0.0, x)], axis=0)


def _rwkv_kernel(r_ref, lw_ref, k_ref, v_ref, a_ref, b_ref, s0_ref, y_o, st_o, st_s, *, nb):
    c = pl.program_id(1)

    @pl.when(c == 0)
    def _():
        st_s[...] = s0_ref[...]

    n2 = 2 * SCAN
    ti = lax.broadcasted_iota(jnp.int32, (SCAN, SCAN), 0)
    tj = lax.broadcasted_iota(jnp.int32, (SCAN, SCAN), 1)
    tri = jnp.where(ti >= tj, 1.0, 0.0).astype(BF16)
    ri = lax.broadcasted_iota(jnp.int32, (n2, n2), 0) % SCAN
    ci = lax.broadcasted_iota(jnp.int32, (n2, n2), 1) % SCAN
    strict = ri > ci
    incl = ri >= ci
    lo_mask = lax.broadcasted_iota(jnp.int32, (SCAN, PAIR), 1) < HD

    chains = [(i, j) for i in range(nb) for j in range(N_PAIR)]
    sl = lambda j: slice(j * PAIR, (j + 1) * PAIR)
    sts = [st_s[i, j] for i, j in chains]
    ins = [[ref[i, :, sl(j)] for ref in (r_ref, k_ref, v_ref, a_ref, b_ref)] for i, j in chains]

    cums, lws = [], []
    for i in range(nb):
        lw_all = lw_ref[i]
        l1 = lw_all.astype(BF16)
        d1 = lw_all - l1.astype(F32)
        l2 = d1.astype(BF16)
        l3 = (d1 - l2.astype(F32)).astype(BF16)
        cum_all = (jnp.dot(tri, l1, preferred_element_type=F32) + jnp.dot(tri, l2, preferred_element_type=F32)
                   + jnp.dot(tri, l3, preferred_element_type=F32))
        for j in range(N_PAIR):
            cums.append(cum_all[:, sl(j)])
            lws.append(lw_all[:, sl(j)])

    def prep(n):
        r, k, v, a, b = ins[n]
        cum = cums[n]
        cum_c = cum[SCAN - 1:SCAN, :]
        e_n = jnp.exp(-cum)
        e_c = jnp.exp(cum_c - cum)
        return dict(
            p_c=jnp.exp(cum_c),
            ls=_stack(a * jnp.exp(cum - lws[n]), lo_mask).astype(BF16),
            rs=_stack(r * jnp.exp(cum), lo_mask),
            bs=_stack(b * e_n, lo_mask).astype(BF16), ks=_stack(k * e_n, lo_mask).astype(BF16),
            vs=_stack(v.astype(F32), lo_mask).astype(BF16),
            bh=_stack(b * e_c, lo_mask).astype(BF16), kh=_stack(k * e_c, lo_mask).astype(BF16))

    P = [prep(n) for n in range(len(chains))]
    aas = [_dot_nt(jnp.concatenate([p["ls"], p["rs"].astype(BF16)], axis=0),
                   jnp.concatenate([p["bs"], p["ks"]], axis=0)) for p in P]
    pws = [jnp.where(strict, aa[:n2, :n2], 0.0) for aa in aas]
    aks = [jnp.where(strict, aa[:n2, n2:], 0.0) for aa in aas]
    rbs = [jnp.where(incl, aa[n2:, :n2], 0.0) for aa in aas]
    rks = [jnp.where(incl, aa[n2:, n2:], 0.0) for aa in aas]

    xs = [jnp.concatenate([p["ls"].astype(F32), _dot(ak, p["vs"])], axis=1) for p, ak in zip(P, aks)]
    for it in range(6):
        xs = [x + _dot(pw, x) for x, pw in zip(xs, pws)]
        if it < 5:
            pws = [_dot(pw, pw) for pw in pws]

    ys_out, st_new = [], []
    for n, p in enumerate(P):
        wl = xs[n][:, :PAIR]
        uv = jnp.concatenate([xs[n][:, PAIR:], p["vs"].astype(F32)], axis=0)
        rq = p["rs"] + _dot(rbs[n], wl)
        y0 = _dot(jnp.concatenate([rbs[n], rks[n]], axis=1), uv)
        g_m = _dot(wl.T, p["bh"])
        h_m = _dot(uv.T, jnp.concatenate([p["bh"], p["kh"]], axis=0))
        ys = _dot_nt(rq, sts[n]) + y0
        ys_out.append(ys[:SCAN] + ys[SCAN:])
        st_new.append(p["p_c"] * sts[n] + _dot_x2(sts[n], g_m) + h_m)

    for i in range(nb):
        y_o[i] = jnp.concatenate(ys_out[i * N_PAIR:(i + 1) * N_PAIR], axis=1)
    for n, (i, j) in enumerate(chains):
        st_s[i, j] = st_new[n]
        st_o[i, j] = st_new[n]


def _rwkv(ins, s0):
    B, T, _ = ins[0].shape
    nc = T // SCAN
    nb = RWKV_ROWS if B % RWKV_ROWS == 0 else 1
    tok = pl.BlockSpec((nb, SCAN, RW), lambda b, c: (b, c, 0))
    st_spec = pl.BlockSpec((nb, N_PAIR, PAIR, PAIR), lambda b, c: (b, 0, 0, 0))
    return pl.pallas_call(
        functools.partial(_rwkv_kernel, nb=nb),
        grid=(B // nb, nc), in_specs=[tok] * 6 + [st_spec], out_specs=[tok, st_spec],
        out_shape=[jax.ShapeDtypeStruct((B, T, RW), F32), jax.ShapeDtypeStruct((B, N_PAIR, PAIR, PAIR), F32)],
        scratch_shapes=[pltpu.VMEM((nb, N_PAIR, PAIR, PAIR), F32)],
        compiler_params=pltpu.CompilerParams(dimension_semantics=("arbitrary", "arbitrary"),
                                             vmem_limit_bytes=VMEM_LIMIT),
        name="rwkv_scan",
    )(*ins, s0)


def _attn_kernel(q_ref, k_ref, kx_ref, wuv_ref, o_ref, m_s, l_s, acc_s, *, tq, n_main, n_valid_extra):
    i = pl.program_id(1)
    rows = HEADS * tq
    q = q_ref[0].reshape(rows, QK_W)
    m_s[...] = jnp.full(m_s.shape, NEG, F32)
    l_s[...] = jnp.zeros(l_s.shape, F32)
    acc_s[...] = jnp.zeros(acc_s.shape, F32)

    def group(kb, fix):
        s = lax.dot_general(q, kb, (((1,), (1,)), ((), ())), preferred_element_type=F32)
        if fix is not None:
            s = fix(s)
        m_prev = m_s[...]
        m_next = jnp.maximum(m_prev, jnp.max(s, axis=1, keepdims=True))
        alpha = jnp.exp2(m_prev - m_next)
        pr = jnp.exp2(s - jnp.concatenate([m_next] * (kb.shape[0] // LANE), axis=1))
        l_s[...] = alpha * l_s[...] + jnp.sum(pr, axis=1, keepdims=True)
        acc_s[...] = alpha * acc_s[...] + jnp.dot(pr.astype(BF16), kb[:, :KV_LORA], preferred_element_type=F32)
        m_s[...] = m_next

    if n_main:
        def body(g, carry):
            group(k_ref[0, pl.ds(pl.multiple_of(g * tq, tq), tq), :], None)
            return carry

        lax.fori_loop(0, i, body, 0)
        k_diag = k_ref[0, pl.ds(pl.multiple_of(i * tq, tq), tq), :]
        k_extra = kx_ref[...]
        rc = (lax.broadcasted_iota(jnp.int32, (rows, tq), 0) % tq) // CHUNK
        cc = lax.broadcasted_iota(jnp.int32, (rows, tq), 1) // CHUNK
        diag_ok = cc <= rc
        extra_ok = lax.broadcasted_iota(jnp.int32, (rows, LANE), 1) < n_valid_extra

        def fix(s):
            return jnp.concatenate([jnp.where(diag_ok, s[:, :tq], NEG), jnp.where(extra_ok, s[:, tq:], NEG)], axis=1)

        group(jnp.concatenate([k_diag, k_extra], axis=0), fix)
    else:
        n_groups = k_ref.shape[1] // ATT_GROUP
        for g in range(n_groups):
            valid = min(ATT_GROUP, n_valid_extra - g * ATT_GROUP)
            fix = None
            if valid < ATT_GROUP:
                ok = lax.broadcasted_iota(jnp.int32, (rows, ATT_GROUP), 1) < valid
                fix = lambda s, ok=ok: jnp.where(ok, s, NEG)
            group(k_ref[0, g * ATT_GROUP:(g + 1) * ATT_GROUP, :], fix)

    lat = acc_s[...] / l_s[...]
    lat_wide = jnp.concatenate([lat[hh * tq:(hh + 1) * tq] for hh in range(HEADS)], axis=1).astype(BF16)
    o_ref[0] = jnp.dot(lat_wide, wuv_ref[...], preferred_element_type=F32)


def _attention(q, keys, keys_extra, wuv, tq, n_main, n_valid_extra):
    B, _, T, _ = q.shape
    tkeys = keys.shape[1]
    kern = functools.partial(_attn_kernel, tq=tq, n_main=n_main, n_valid_extra=n_valid_extra)
    rows = HEADS * tq
    return pl.pallas_call(
        kern,
        grid=(B, T // tq),
        in_specs=[pl.BlockSpec((1, HEADS, tq, QK_W), lambda b, i: (b, 0, i, 0)),
                  pl.BlockSpec((1, tkeys, QK_W), lambda b, i: (b, 0, 0)),
                  _full(keys_extra.shape), _full(wuv.shape)],
        out_specs=pl.BlockSpec((1, tq, RW), lambda b, i: (b, i, 0)),
        out_shape=jax.ShapeDtypeStruct((B, T, RW), F32),
        scratch_shapes=[pltpu.VMEM((rows, LANE), F32)] * 3,
        compiler_params=pltpu.CompilerParams(dimension_semantics=("arbitrary", "arbitrary"),
                                             vmem_limit_bytes=VMEM_LIMIT),
        name="mla_attention",
    )(q, keys, keys_extra, wuv)


def _stage_d_kernel(x1_ref, y_ref, bon_ref, g_ref, mla_ref, ones_ref, lnw_ref, lnb_ref, wo_ref,
                    n2_ref, w1_ref, w3_ref, w2_ref, fn_ref, o_ref):
    y = y_ref[...]
    mu = _dot_x2(y, ones_ref[...]) * (1.0 / HD)
    d = y - mu
    var = _dot(d * d, ones_ref[...]) * (1.0 / HD)
    yn = d * lax.rsqrt(var + GN_EPS) * lnw_ref[...] + lnb_ref[...]
    rw = ((yn + bon_ref[...]) * g_ref[...]).astype(BF16)
    x2 = (x1_ref[...] + jnp.dot(rw, wo_ref[:RW, :], preferred_element_type=F32)
          + jnp.dot(mla_ref[...].astype(BF16), wo_ref[RW:, :], preferred_element_type=F32))
    h = _rms(x2, n2_ref[...]).astype(BF16)
    x3 = x2 + 0.5 * _ffn(h, w1_ref, w3_ref, w2_ref)
    o_ref[...] = _rms(x3, fn_ref[...])


def _stage_d(x1, y, bon, g, mla, W, tm):
    B, T, _ = x1.shape
    n = B * T
    flat = lambda t: t.reshape(n, t.shape[-1])
    tok = lambda w: pl.BlockSpec((tm, w), lambda i: (i, 0))
    weights = [W["ones"], W["lnw"], W["lnb"], W["wout"], W["n2"], W["f2w1"], W["f2w3"], W["f2w2"], W["fnorm"]]
    out = pl.pallas_call(
        _stage_d_kernel,
        grid=(n // tm,),
        in_specs=[tok(D_MODEL), tok(RW), tok(RW), tok(RW), tok(RW)] + [_full(w.shape) for w in weights],
        out_specs=tok(D_MODEL),
        out_shape=jax.ShapeDtypeStruct((n, D_MODEL), F32),
        compiler_params=pltpu.CompilerParams(dimension_semantics=("arbitrary",), vmem_limit_bytes=VMEM_LIMIT),
        name="stage_d",
    )(flat(x1), flat(y), flat(bon), flat(g), flat(mla), *weights)
    return out.reshape(B, T, D_MODEL)


def _rot_cols(w):
    half = ROPE // 2
    return jnp.concatenate([-w[..., half:], w[..., :half]], axis=-1)


def _prep_weights(norm_ffn1, ffn1_w1, ffn1_w3, ffn1_w2, norm_mix, w_in, mu_shift, w0, w_w2, a0, w_a2, w_g2, k_k,
                  k_a, r_k, ln_x_w, ln_x_b, q_norm, w_q_up, kv_norm, w_uk, w_uv, w_out, norm_ffn2, ffn2_w1,
                  ffn2_w3, ffn2_w2, final_norm):
    row = lambda t: t.reshape(1, -1).astype(F32)
    wi = w_in[0]
    wkr = wi[:, O_KR:O_KR + ROPE]
    win = jnp.concatenate([wi, _rot_cols(wkr), jnp.zeros((D_MODEL, LANE - 2 * ROPE), F32)], axis=1)
    zl = jnp.zeros((HD, RW), F32)
    wq = w_q_up[0]
    wq_rope = wq[:, :, HD:]
    wqr = jnp.concatenate([wq_rope, _rot_cols(wq_rope), jnp.zeros((Q_LORA, HEADS, LANE - 2 * ROPE), F32)], axis=-1)
    eye = jnp.eye(HEADS, dtype=F32)
    wuk_bd = jnp.einsum("chd,hg->hdgc", w_uk[0], eye).reshape(HEADS * HD, HEADS * KV_LORA)
    wuv_bd = jnp.einsum("chd,hg->hcgd", w_uv[0], eye).reshape(HEADS * KV_LORA, RW)
    ones_bd = jnp.kron(eye, jnp.ones((HD, HD), F32))
    return dict(
        n1=row(norm_ffn1), f1w1=ffn1_w1[0].astype(BF16), f1w3=ffn1_w3[0].astype(BF16), f1w2=ffn1_w2[0].astype(BF16),
        nmix=row(norm_mix), win=win.astype(BF16), mu=row(mu_shift),
        w0=row(w0), ww2=jnp.concatenate([w_w2[0], zl], axis=0).astype(BF16), a0=row(a0),
        wa2=jnp.concatenate([zl, w_a2[0]], axis=0).astype(BF16), wg2=w_g2[0].astype(BF16),
        kk=row(k_k), ka=row(k_a), rk=row(r_k), ones=ones_bd.astype(BF16),
        qn=row(q_norm), wqn=wq[:, :, :HD].reshape(Q_LORA, RW).astype(BF16),
        wqr=wqr.reshape(Q_LORA, HEADS * LANE).astype(BF16),
        kvn=row(kv_norm), wuk=wuk_bd.astype(BF16), wuv=wuv_bd.astype(BF16),
        lnw=row(ln_x_w), lnb=row(ln_x_b), wout=w_out[0].astype(BF16),
        n2=row(norm_ffn2), f2w1=ffn2_w1[0].astype(BF16), f2w3=ffn2_w3[0].astype(BF16),
        f2w2=ffn2_w2[0].astype(BF16), fnorm=row(final_norm))


def _rope_tables(pos):
    half = ROPE // 2
    inv = ROPE_THETA ** (-jnp.arange(half, dtype=F32) / half)
    ang = pos.astype(F32)[:, None] * inv[None, :]
    pad = lambda t: jnp.pad(jnp.concatenate([t, t], axis=1), ((0, 0), (0, LANE - ROPE)))
    return pad(jnp.cos(ang)), pad(jnp.sin(ang))


def _state_to_pairs(s):
    B = s.shape[0]
    s = s.reshape(B, N_PAIR, 2, HD, HD)
    z = jnp.zeros_like(s[:, :, 0])
    top = jnp.concatenate([s[:, :, 0], z], axis=-1)
    bot = jnp.concatenate([z, s[:, :, 1]], axis=-1)
    return jnp.concatenate([top, bot], axis=-2)


def _pairs_to_state(s):
    B = s.shape[0]
    return jnp.stack([s[:, :, :HD, :HD], s[:, :, HD:, HD:]], axis=2).reshape(B, HEADS, HD, HD)


def _pad_rows(t, n):
    return jnp.pad(t, ((0, 0), (0, n - t.shape[1]), (0, 0)))


def _tile(n, pref):
    return pref if n % pref == 0 else n


def kernel(x_prompt, x_sample, cache_ckv, cache_krope, state_wkv, state_shift, meta_tokens, norm_ffn1, ffn1_w1, ffn1_w3, ffn1_w2, norm_mix, w_in, mu_shift, w0, w_w2, a0, w_a2, w_g2, k_k, k_a, r_k, ln_x_w, ln_x_b, q_norm, w_q_up, kv_norm, w_uk, w_uv, w_out, norm_ffn2, ffn2_w1, ffn2_w3, ffn2_w2, final_norm):
    assert w_in.shape[0] == 1, "single-layer stack"
    B, S, _ = x_prompt.shape
    Bd, Tn, _ = x_sample.shape
    past = cache_ckv.shape[2]
    assert S % ATT_GROUP == 0 and Tn % SCAN == 0
    W = _prep_weights(norm_ffn1, ffn1_w1, ffn1_w3, ffn1_w2, norm_mix, w_in, mu_shift, w0, w_w2, a0, w_a2, w_g2,
                      k_k, k_a, r_k, ln_x_w, ln_x_b, q_norm, w_q_up, kv_norm, w_uk, w_uv, w_out, norm_ffn2,
                      ffn2_w1, ffn2_w3, ffn2_w2, final_norm)
    scan_keys = ("r", "lw", "k", "v", "a", "b")

    cos_m, sin_m = _rope_tables(jnp.arange(N_META))
    xm = meta_tokens[None].astype(F32)
    am = _stage_a(_ffn1(xm, W, N_META), jnp.zeros((1, 1, RWC), F32), cos_m, sin_m, W, N_META)
    _, st_m = _rwkv([_pad_rows(am[n], SCAN) for n in scan_keys], jnp.zeros((1, N_PAIR, PAIR, PAIR), F32))

    tm = _tile(S, 512)
    cos_p, sin_p = _rope_tables(N_META + jnp.arange(S))
    x1_p = _ffn1(x_prompt, W, _tile(B * S, 1024))
    ap = _stage_a(x1_p, jnp.broadcast_to(am["shift"], (B, 1, RWC)), cos_p, sin_p, W, tm)
    y_p, st_p = _rwkv([ap[n] for n in scan_keys], jnp.broadcast_to(st_m, (B, N_PAIR, PAIR, PAIR)))
    keys_m = _pad_rows(am["kcat"], LANE)[0]
    mla_p = _attention(ap["q"], ap["kcat"], keys_m, W["wuv"], ATT_GROUP, S, N_META)
    y_prompt = _stage_d(x1_p, y_p, ap["bon"], ap["g"], mla_p, W, tm)

    cos_s, sin_s = _rope_tables(N_META + past + jnp.arange(Tn))
    x1_s = _ffn1(x_sample, W, _tile(Bd * Tn, 512))
    as_ = _stage_a(x1_s, state_shift[0].astype(F32), cos_s, sin_s, W, Tn)
    y_s, st_s = _rwkv([as_[n] for n in scan_keys], _state_to_pairs(state_wkv[0].astype(F32)))
    cache_k = jnp.concatenate([cache_ckv[0], cache_krope[0], jnp.zeros((Bd, past, QK_W - KV_LORA - ROPE), F32)],
                              axis=-1).astype(BF16)
    n_valid = past + Tn + N_META
    n_keys = -(-n_valid // ATT_GROUP) * ATT_GROUP
    keys_s = jnp.concatenate([cache_k, as_["kcat"], jnp.broadcast_to(am["kcat"], (Bd, N_META, QK_W)),
                              jnp.zeros((Bd, n_keys - n_valid, QK_W), BF16)], axis=1)
    mla_s = _attention(as_["q"], keys_s, keys_m, W["wuv"], Tn, 0, n_valid)
    y_sample = _stage_d(x1_s, y_s, as_["bon"], as_["g"], mla_s, W, _tile(Bd * Tn, 512))

    bc = lambda t, n: jnp.broadcast_to(t, (n,) + t.shape[1:])
    ckv_p = jnp.concatenate([bc(am["c"], B), ap["c"]], axis=1)[None]
    kr_p = jnp.concatenate([bc(am["kr"], B), ap["kr"]], axis=1)[None, ..., :ROPE]
    return (y_prompt, y_sample, ckv_p, kr_p, _pairs_to_state(st_p)[None], ap["shift"][None],
            as_["c"][None], as_["kr"][None, ..., :ROPE], _pairs_to_state(st_s)[None], as_["shift"][None])
```

```python
import functools

import jax
import jax.numpy as jnp
from jax import lax
from jax.experimental import pallas as pl
from jax.experimental.pallas import tpu as pltpu

F32 = jnp.float32
BF16 = jnp.bfloat16

D_MODEL = 1024
D_FF = 2816
CHUNK = 64
N_META = 16
HEADS = 8
HD = 64
RW = HEADS * HD
ROPE = 32
Q_LORA = 256
KV_LORA = 128
NORM_EPS = 1e-6
GN_EPS = 64e-5
MLA_SCALE = (HD + ROPE) ** -0.5
ROPE_THETA = 10000.0

LANE = 128
SCAN = 64
PAIR = 2 * HD
N_PAIR = HEADS // 2
RWKV_ROWS = 4
QK_W = 256

O_R, O_K, O_V, O_WA, O_GL = 0, 512, 1024, 1536, 1664
RWC = 1792
O_Q, O_KV, O_KR = 1792, 2048, 2176
PW = 2304
ROT_SHIFT = LANE - ROPE
FF_CHUNK = 256
VMEM_LIMIT = 56 * 1024 * 1024
NEG = -1e30
LOG2E = 1.4426950408889634
ATT_ROWS = 1024
ATT_GROUP = 512


def _dot(a, b):
    return jnp.dot(a.astype(BF16), b.astype(BF16), preferred_element_type=F32)


def _dot_nt(a, b):
    return lax.dot_general(a.astype(BF16), b.astype(BF16), (((1,), (1,)), ((), ())),
                           preferred_element_type=F32)


def _split2(x):
    hi = x.astype(BF16)
    lo = (x - hi.astype(F32)).astype(BF16)
    return hi, lo


def _dot_x2(a, b):
    hi, lo = _split2(a)
    bb = b.astype(BF16)
    return (jnp.dot(hi, bb, preferred_element_type=F32) + jnp.dot(lo, bb, preferred_element_type=F32))


def _rms(x, g):
    return x * lax.rsqrt(jnp.mean(x * x, axis=-1, keepdims=True) + NORM_EPS) * g


def _ffn(h, w1_ref, w3_ref, w2_ref):
    acc = None
    for c in range(D_FF // FF_CHUNK):
        sl = slice(c * FF_CHUNK, (c + 1) * FF_CHUNK)
        a = jnp.dot(h, w1_ref[:, sl], preferred_element_type=F32)
        b = jnp.dot(h, w3_ref[:, sl], preferred_element_type=F32)
        z = (a * jax.nn.sigmoid(a) * b).astype(BF16)
        t = jnp.dot(z, w2_ref[sl, :], preferred_element_type=F32)
        acc = t if acc is None else acc + t
    return acc


def _ffn1_kernel(x_ref, n1_ref, w1_ref, w3_ref, w2_ref, o_ref):
    x = x_ref[...]
    h = _rms(x, n1_ref[...]).astype(BF16)
    o_ref[...] = x + 0.5 * _ffn(h, w1_ref, w3_ref, w2_ref)


def _full(shape):
    return pl.BlockSpec(memory_space=pltpu.VMEM)


def _ffn1(x, W, tm):
    B, T, _ = x.shape
    n = B * T
    tok = pl.BlockSpec((tm, D_MODEL), lambda i: (i, 0))
    weights = [W["n1"], W["f1w1"], W["f1w3"], W["f1w2"]]
    out = pl.pallas_call(
        _ffn1_kernel,
        grid=(n // tm,),
        in_specs=[tok] + [_full(w.shape) for w in weights],
        out_specs=tok,
        out_shape=jax.ShapeDtypeStruct((n, D_MODEL), F32),
        compiler_params=pltpu.CompilerParams(dimension_semantics=("arbitrary",), vmem_limit_bytes=VMEM_LIMIT),
        name="ffn1",
    )(x.reshape(n, D_MODEL), *weights)
    return out.reshape(B, T, D_MODEL)


def _rope(blk, cos, sin):
    return blk * cos + pltpu.roll(blk, ROT_SHIFT, 1) * sin


def _stage_a_kernel(x1_ref, sh0_ref, cos_ref, sin_ref,
                    nmix_ref, win_ref, mu_ref,
                    w0_ref, ww2_ref, a0_ref, wa2_ref, wg2_ref, kk_ref, ka_ref, rk_ref, ones_ref,
                    qn_ref, wqn_ref, wqr_ref, kvn_ref, wuk_ref,
                    r_o, lw_o, k_o, v_o, a_o, b_o, bon_o, g_o, c_o, kr_o, kcat_o, q_o, sh_o,
                    prev_s, *, tm, n_sub):
    t = pl.program_id(1)

    @pl.when(t == 0)
    def _():
        prev_s[...] = sh0_ref[0]

    ts = tm // n_sub

    def rows_block(rows, prev_row):
        hm = _rms(x1_ref[0, rows, :], nmix_ref[...]).astype(BF16)
        p = jnp.dot(hm, win_ref[...], preferred_element_type=F32)

        prw = p[:, :RWC]
        rolled = pltpu.roll(prw, 1, 0)
        row = lax.broadcasted_iota(jnp.int32, prw.shape, 0)
        prev = jnp.where(row == 0, prev_row, rolled)
        s = prw + mu_ref[...] * (prev - prw)

        r = s[:, O_R:O_R + RW]
        k = s[:, O_K:O_K + RW]
        v = s[:, O_V:O_V + RW]
        wa = s[:, O_WA:O_WA + LANE]
        gl = s[:, O_GL:O_GL + LANE]
        z = w0_ref[...] + _dot(jnp.tanh(wa), ww2_ref[...])
        softplus = jnp.maximum(-z, 0.0) + jnp.log(1.0 + jnp.exp(-jnp.abs(z)))
        logw = -softplus - 0.5
        a_sig = jax.nn.sigmoid(a0_ref[...] + _dot(wa, wa2_ref[...]))
        kk = k * kk_ref[...]
        ss = _dot(kk * kk, ones_ref[...])
        kk = kk / jnp.maximum(jnp.sqrt(ss), 1e-12)
        k2 = k * (1.0 + (a_sig - 1.0) * ka_ref[...])

        cos = cos_ref[rows, :]
        sin = sin_ref[rows, :]
        c = _rms(p[:, O_KV:O_KV + KV_LORA], kvn_ref[...])
        kr = _rope(p[:, O_KR:O_KR + LANE], cos, sin)
        cq = _rms(p[:, O_Q:O_Q + Q_LORA], qn_ref[...]).astype(BF16)
        qn = jnp.dot(cq, wqn_ref[...], preferred_element_type=F32)
        qlat = _dot(qn, wuk_ref[...])
        qr = jnp.dot(cq, wqr_ref[...], preferred_element_type=F32)
        q = []
        for hh in range(HEADS):
            sl = slice(hh * LANE, (hh + 1) * LANE)
            qrh = _rope(qr[:, sl], cos, sin)
            q.append((jnp.concatenate([qlat[:, sl], qrh], axis=1) * (MLA_SCALE * LOG2E)).astype(BF16))
        outs = dict(r=r, lw=-jnp.exp(logw), k=k2, v=v.astype(BF16), a=-kk, b=kk * a_sig,
                    bon=_dot(r * k2 * rk_ref[...], ones_ref[...]) * v, g=_dot(jax.nn.sigmoid(gl), wg2_ref[...]),
                    c=c, kr=kr, kcat=jnp.concatenate([c, kr], axis=1).astype(BF16), q=q)
        return outs, prw[ts - 1:ts, :]

    prev_row = prev_s[...]
    done = []
    for i in range(n_sub):
        rows = slice(i * ts, (i + 1) * ts)
        outs, prev_row = rows_block(rows, prev_row)
        done.append((rows, outs))
    prev_s[...] = prev_row
    sh_o[0] = prev_row
    refs = dict(r=r_o, lw=lw_o, k=k_o, v=v_o, a=a_o, b=b_o, bon=bon_o, g=g_o, c=c_o, kr=kr_o, kcat=kcat_o)
    for rows, outs in done:
        for name, ref in refs.items():
            ref[0, rows, :] = outs[name]
        for hh in range(HEADS):
            q_o[0, hh, rows, :] = outs["q"][hh]


def _stage_a(x1, sh0, cos, sin, W, tm):
    B, T, _ = x1.shape
    nt = T // tm
    tok = lambda w: pl.BlockSpec((1, tm, w), lambda b, t: (b, t, 0))
    weights = [W["nmix"], W["win"], W["mu"],
               W["w0"], W["ww2"], W["a0"], W["wa2"], W["wg2"], W["kk"], W["ka"], W["rk"], W["ones"],
               W["qn"], W["wqn"], W["wqr"], W["kvn"], W["wuk"]]
    in_specs = ([tok(D_MODEL), pl.BlockSpec((1, 1, RWC), lambda b, t: (b, 0, 0)),
                 pl.BlockSpec((tm, LANE), lambda b, t: (t, 0)), pl.BlockSpec((tm, LANE), lambda b, t: (t, 0))]
                + [_full(w.shape) for w in weights])
    f = lambda w, dt=F32: jax.ShapeDtypeStruct((B, T, w), dt)
    names = ["r", "lw", "k", "v", "a", "b", "bon", "g", "c", "kr", "kcat", "q", "shift"]
    out_shape = [f(RW), f(RW), f(RW), f(RW, BF16), f(RW), f(RW), f(RW), f(RW), f(KV_LORA), f(LANE), f(QK_W, BF16),
                 jax.ShapeDtypeStruct((B, HEADS, T, QK_W), BF16), jax.ShapeDtypeStruct((B, 1, RWC), F32)]
    out_specs = ([tok(RW)] * 8 + [tok(KV_LORA), tok(LANE), tok(QK_W),
                                  pl.BlockSpec((1, HEADS, tm, QK_W), lambda b, t: (b, 0, t, 0)),
                                  pl.BlockSpec((1, 1, RWC), lambda b, t: (b, 0, 0))])
    outs = pl.pallas_call(
        functools.partial(_stage_a_kernel, tm=tm, n_sub=2 if tm % 512 == 0 else 1),
        grid=(B, nt), in_specs=in_specs, out_specs=out_specs, out_shape=out_shape,
        scratch_shapes=[pltpu.VMEM((1, RWC), F32)],
        compiler_params=pltpu.CompilerParams(dimension_semantics=("arbitrary", "arbitrary"),
                                             vmem_limit_bytes=VMEM_LIMIT),
        name="stage_a",
    )(x1, sh0, cos, sin, *weights)
    return dict(zip(names, outs))


def _stack(x, lo_mask):
    return jnp.concatenate([jnp.where(lo_mask, x, 0.0), jnp.where(lo_mask, 0.0, x)], axis=0)


def _rwkv_kernel(r_ref, lw_ref, k_ref, v_ref, a_ref, b_ref, s0_ref, y_o, st_o, st_s, *, nb):
    c = pl.program_id(1)

    @pl.when(c == 0)
    def _():
        st_s[...] = s0_ref[...]

    n2 = 2 * SCAN
    ti = lax.broadcasted_iota(jnp.int32, (SCAN, SCAN), 0)
    tj = lax.broadcasted_iota(jnp.int32, (SCAN, SCAN), 1)
    tri = jnp.where(ti >= tj, 1.0, 0.0).astype(BF16)
    ri = lax.broadcasted_iota(jnp.int32, (n2, n2), 0) % SCAN
    ci = lax.broadcasted_iota(jnp.int32, (n2, n2), 1) % SCAN
    strict = ri > ci
    incl = ri >= ci
    lo_mask = lax.broadcasted_iota(jnp.int32, (SCAN, PAIR), 1) < HD

    chains = [(i, j) for i in range(nb) for j in range(N_PAIR)]
    sl = lambda j: slice(j * PAIR, (j + 1) * PAIR)
    sts = [st_s[i, j] for i, j in chains]
    ins = [[ref[i, :, sl(j)] for ref in (r_ref, k_ref, v_ref, a_ref, b_ref)] for i, j in chains]

    cums, lws = [], []
    for i in range(nb):
        lw_all = lw_ref[i]
        l1 = lw_all.astype(BF16)
        d1 = lw_all - l1.astype(F32)
        l2 = d1.astype(BF16)
        l3 = (d1 - l2.astype(F32)).astype(BF16)
        cum_all = (jnp.dot(tri, l1, preferred_element_type=F32) + jnp.dot(tri, l2, preferred_element_type=F32)
                   + jnp.dot(tri, l3, preferred_element_type=F32))
        for j in range(N_PAIR):
            cums.append(cum_all[:, sl(j)])
            lws.append(lw_all[:, sl(j)])

    def prep(n):
        r, k, v, a, b = ins[n]
        cum = cums[n]
        cum_c = cum[SCAN - 1:SCAN, :]
        e_n = jnp.exp(-cum)
        e_c = jnp.exp(cum_c - cum)
        return dict(
            p_c=jnp.exp(cum_c),
            ls=_stack(a * jnp.exp(cum - lws[n]), lo_mask).astype(BF16),
            rs=_stack(r * jnp.exp(cum), lo_mask),
            bs=_stack(b * e_n, lo_mask).astype(BF16), ks=_stack(k * e_n, lo_mask).astype(BF16),
            vs=_stack(v.astype(F32), lo_mask).astype(BF16),
            bh=_stack(b * e_c, lo_mask).astype(BF16), kh=_stack(k * e_c, lo_mask).astype(BF16))

    P = [prep(n) for n in range(len(chains))]
    aas = [_dot_nt(jnp.concatenate([p["ls"], p["rs"].astype(BF16)], axis=0),
                   jnp.concatenate([p["bs"], p["ks"]], axis=0)) for p in P]
    pws = [jnp.where(strict, aa[:n2, :n2], 0.0) for aa in aas]
    aks = [jnp.where(strict, aa[:n2, n2:], 0.0) for aa in aas]
    rbs = [jnp.where(incl, aa[n2:, :n2], 0.0) for aa in aas]
    rks = [jnp.where(incl, aa[n2:, n2:], 0.0) for aa in aas]

    xs = [jnp.concatenate([p["ls"].astype(F32), _dot(ak, p["vs"])], axis=1) for p, ak in zip(P, aks)]
    for it in range(6):
        xs = [x + _dot(pw, x) for x, pw in zip(xs, pws)]
        if it < 5:
            pws = [_dot(pw, pw) for pw in pws]

    wls = [x[:, :PAIR] for x in xs]
    uvs = [jnp.concatenate([x[:, PAIR:], p["vs"].astype(F32)], axis=0) for x, p in zip(xs, P)]
    rqs = [p["rs"] + _dot(rb, wl) for p, rb, wl in zip(P, rbs, wls)]
    y0s = [_dot(jnp.concatenate([rb, rk], axis=1), uv) for rb, rk, uv in zip(rbs, rks, uvs)]
    gms = [_dot(wl.T, p["bh"]) for wl, p in zip(wls, P)]
    hms = [_dot(uv.T, jnp.concatenate([p["bh"], p["kh"]], axis=0)) for uv, p in zip(uvs, P)]
    yss = [_dot_nt(rq, st) + y0 for rq, st, y0 in zip(rqs, sts, y0s)]
    ys_out = [ys[:SCAN] + ys[SCAN:] for ys in yss]
    st_new = [p["p_c"] * st + _dot_x2(st, g_m) + h_m for p, st, g_m, h_m in zip(P, sts, gms, hms)]

    for i in range(nb):
        y_o[i] = jnp.concatenate(ys_out[i * N_PAIR:(i + 1) * N_PAIR], axis=1)
    for n, (i, j) in enumerate(chains):
        st_s[i, j] = st_new[n]
        st_o[i, j] = st_new[n]


def _rwkv(ins, s0):
    B, T, _ = ins[0].shape
    nc = T // SCAN
    nb = RWKV_ROWS if B % RWKV_ROWS == 0 else 1
    tok = pl.BlockSpec((nb, SCAN, RW), lambda b, c: (b, c, 0))
    st_spec = pl.BlockSpec((nb, N_PAIR, PAIR, PAIR), lambda b, c: (b, 0, 0, 0))
    return pl.pallas_call(
        functools.partial(_rwkv_kernel, nb=nb),
        grid=(B // nb, nc), in_specs=[tok] * 6 + [st_spec], out_specs=[tok, st_spec],
        out_shape=[jax.ShapeDtypeStruct((B, T, RW), F32), jax.ShapeDtypeStruct((B, N_PAIR, PAIR, PAIR), F32)],
        scratch_shapes=[pltpu.VMEM((nb, N_PAIR, PAIR, PAIR), F32)],
        compiler_params=pltpu.CompilerParams(dimension_semantics=("arbitrary", "arbitrary"),
                                             vmem_limit_bytes=VMEM_LIMIT),
        name="rwkv_scan",
    )(*ins, s0)


def _attn_kernel(q_ref, k_ref, kx_ref, wuv_ref, o_ref, m_s, l_s, acc_s, *, tq, n_main, n_valid_extra):
    i = pl.program_id(1)
    rows = HEADS * tq
    q = q_ref[0].reshape(rows, QK_W)
    def group(kb, fix, blk, first=False):
        vb = kb[:, :KV_LORA]
        reps = kb.shape[0] // LANE

        def chain(rb):
            s = lax.dot_general(q[rb], kb, (((1,), (1,)), ((), ())), preferred_element_type=F32)
            if fix is not None:
                s = fix(s)
            yield
            m_cur = jnp.max(s, axis=1, keepdims=True)
            if first:
                m_next = jnp.broadcast_to(m_cur, (blk, LANE))
            else:
                m_prev = m_s[rb, :]
                m_next = jnp.maximum(m_prev, m_cur)
                alpha = jnp.exp2(m_prev - m_next)
            p = jnp.exp2(s - jnp.concatenate([m_next] * reps, axis=1))
            l_new = jnp.broadcast_to(jnp.sum(p, axis=1, keepdims=True), (blk, LANE))
            if not first:
                l_new = alpha * l_s[rb, :] + l_new
            yield
            acc_new = jnp.dot(p.astype(BF16), vb, preferred_element_type=F32)
            if not first:
                acc_new = alpha * acc_s[rb, :] + acc_new
            return rb, m_next, l_new, acc_new

        pending = [slice(j * blk, (j + 1) * blk) for j in range(rows // blk)]
        running, done = [], []
        while pending or running:
            if pending:
                running.append(chain(pending.pop(0)))
            for gen in list(running):
                try:
                    next(gen)
                except StopIteration as stop:
                    done.append(stop.value)
                    running.remove(gen)
        for rb, m_next, l_new, acc_new in done:
            m_s[rb, :] = m_next
            l_s[rb, :] = l_new
            acc_s[rb, :] = acc_new

    if n_main:
        def body(g, carry):
            group(k_ref[0, pl.ds(pl.multiple_of(g * tq, tq), tq), :], None, min(rows, ATT_ROWS))
            return carry

        k_diag = k_ref[0, pl.ds(pl.multiple_of(i * tq, tq), tq), :]
        k_extra = kx_ref[...]
        rc = lax.broadcasted_iota(jnp.int32, (tq, tq), 0) // CHUNK
        cc = lax.broadcasted_iota(jnp.int32, (tq, tq), 1) // CHUNK
        diag_ok = cc <= rc
        extra_ok = lax.broadcasted_iota(jnp.int32, (tq, LANE), 1) < n_valid_extra

        def fix(s):
            return jnp.concatenate([jnp.where(diag_ok, s[:, :tq], NEG), jnp.where(extra_ok, s[:, tq:], NEG)], axis=1)

        group(jnp.concatenate([k_diag, k_extra], axis=0), fix, tq, first=True)
        lax.fori_loop(0, i, body, 0)
    else:
        blk = min(rows, ATT_ROWS)
        n_groups = k_ref.shape[1] // ATT_GROUP
        for g in range(n_groups):
            valid = min(ATT_GROUP, n_valid_extra - g * ATT_GROUP)
            fix = None
            if valid < ATT_GROUP:
                ok = lax.broadcasted_iota(jnp.int32, (blk, ATT_GROUP), 1) < valid
                fix = lambda s, ok=ok: jnp.where(ok, s, NEG)
            group(k_ref[0, g * ATT_GROUP:(g + 1) * ATT_GROUP, :], fix, blk, first=g == 0)

    lat = acc_s[...] / l_s[...]
    lat_wide = jnp.concatenate([lat[hh * tq:(hh + 1) * tq] for hh in range(HEADS)], axis=1).astype(BF16)
    o_ref[0] = jnp.dot(lat_wide, wuv_ref[...], preferred_element_type=F32)


def _attention(q, keys, keys_extra, wuv, tq, n_main, n_valid_extra):
    B, _, T, _ = q.shape
    tkeys = keys.shape[1]
    kern = functools.partial(_attn_kernel, tq=tq, n_main=n_main, n_valid_extra=n_valid_extra)
    rows = HEADS * tq
    return pl.pallas_call(
        kern,
        grid=(B, T // tq),
        in_specs=[pl.BlockSpec((1, HEADS, tq, QK_W), lambda b, i: (b, 0, i, 0)),
                  pl.BlockSpec((1, tkeys, QK_W), lambda b, i: (b, 0, 0)),
                  _full(keys_extra.shape), _full(wuv.shape)],
        out_specs=pl.BlockSpec((1, tq, RW), lambda b, i: (b, i, 0)),
        out_shape=jax.ShapeDtypeStruct((B, T, RW), F32),
        scratch_shapes=[pltpu.VMEM((rows, LANE), F32)] * 3,
        compiler_params=pltpu.CompilerParams(dimension_semantics=("arbitrary", "arbitrary"),
                                             vmem_limit_bytes=VMEM_LIMIT),
        name="mla_attention",
    )(q, keys, keys_extra, wuv)


def _stage_d_kernel(x1_ref, y_ref, bon_ref, g_ref, mla_ref, ones_ref, lnw_ref, lnb_ref, wo_ref,
                    n2_ref, w1_ref, w3_ref, w2_ref, fn_ref, o_ref):
    y = y_ref[...]
    mu = _dot_x2(y, ones_ref[...]) * (1.0 / HD)
    d = y - mu
    var = _dot(d * d, ones_ref[...]) * (1.0 / HD)
    yn = d * lax.rsqrt(var + GN_EPS) * lnw_ref[...] + lnb_ref[...]
    rw = ((yn + bon_ref[...]) * g_ref[...]).astype(BF16)
    x2 = (x1_ref[...] + jnp.dot(rw, wo_ref[:RW, :], preferred_element_type=F32)
          + jnp.dot(mla_ref[...].astype(BF16), wo_ref[RW:, :], preferred_element_type=F32))
    h = _rms(x2, n2_ref[...]).astype(BF16)
    x3 = x2 + 0.5 * _ffn(h, w1_ref, w3_ref, w2_ref)
    o_ref[...] = _rms(x3, fn_ref[...])


def _stage_d(x1, y, bon, g, mla, W, tm):
    B, T, _ = x1.shape
    n = B * T
    flat = lambda t: t.reshape(n, t.shape[-1])
    tok = lambda w: pl.BlockSpec((tm, w), lambda i: (i, 0))
    weights = [W["ones"], W["lnw"], W["lnb"], W["wout"], W["n2"], W["f2w1"], W["f2w3"], W["f2w2"], W["fnorm"]]
    out = pl.pallas_call(
        _stage_d_kernel,
        grid=(n // tm,),
        in_specs=[tok(D_MODEL), tok(RW), tok(RW), tok(RW), tok(RW)] + [_full(w.shape) for w in weights],
        out_specs=tok(D_MODEL),
        out_shape=jax.ShapeDtypeStruct((n, D_MODEL), F32),
        compiler_params=pltpu.CompilerParams(dimension_semantics=("arbitrary",), vmem_limit_bytes=VMEM_LIMIT),
        name="stage_d",
    )(flat(x1), flat(y), flat(bon), flat(g), flat(mla), *weights)
    return out.reshape(B, T, D_MODEL)


def _rot_cols(w):
    half = ROPE // 2
    return jnp.concatenate([-w[..., half:], w[..., :half]], axis=-1)


def _prep_weights(norm_ffn1, ffn1_w1, ffn1_w3, ffn1_w2, norm_mix, w_in, mu_shift, w0, w_w2, a0, w_a2, w_g2, k_k,
                  k_a, r_k, ln_x_w, ln_x_b, q_norm, w_q_up, kv_norm, w_uk, w_uv, w_out, norm_ffn2, ffn2_w1,
                  ffn2_w3, ffn2_w2, final_norm):
    row = lambda t: t.reshape(1, -1).astype(F32)
    wi = w_in[0]
    wkr = wi[:, O_KR:O_KR + ROPE]
    win = jnp.concatenate([wi, _rot_cols(wkr), jnp.zeros((D_MODEL, LANE - 2 * ROPE), F32)], axis=1)
    zl = jnp.zeros((HD, RW), F32)
    wq = w_q_up[0]
    wq_rope = wq[:, :, HD:]
    wqr = jnp.concatenate([wq_rope, _rot_cols(wq_rope), jnp.zeros((Q_LORA, HEADS, LANE - 2 * ROPE), F32)], axis=-1)
    eye = jnp.eye(HEADS, dtype=F32)
    wuk_bd = jnp.einsum("chd,hg->hdgc", w_uk[0], eye).reshape(HEADS * HD, HEADS * KV_LORA)
    wuv_bd = jnp.einsum("chd,hg->hcgd", w_uv[0], eye).reshape(HEADS * KV_LORA, RW)
    ones_bd = jnp.kron(eye, jnp.ones((HD, HD), F32))
    return dict(
        n1=row(norm_ffn1), f1w1=ffn1_w1[0].astype(BF16), f1w3=ffn1_w3[0].astype(BF16), f1w2=ffn1_w2[0].astype(BF16),
        nmix=row(norm_mix), win=win.astype(BF16), mu=row(mu_shift),
        w0=row(w0), ww2=jnp.concatenate([w_w2[0], zl], axis=0).astype(BF16), a0=row(a0),
        wa2=jnp.concatenate([zl, w_a2[0]], axis=0).astype(BF16), wg2=w_g2[0].astype(BF16),
        kk=row(k_k), ka=row(k_a), rk=row(r_k), ones=ones_bd.astype(BF16),
        qn=row(q_norm), wqn=wq[:, :, :HD].reshape(Q_LORA, RW).astype(BF16),
        wqr=wqr.reshape(Q_LORA, HEADS * LANE).astype(BF16),
        kvn=row(kv_norm), wuk=wuk_bd.astype(BF16), wuv=wuv_bd.astype(BF16),
        lnw=row(ln_x_w), lnb=row(ln_x_b), wout=w_out[0].astype(BF16),
        n2=row(norm_ffn2), f2w1=ffn2_w1[0].astype(BF16), f2w3=ffn2_w3[0].astype(BF16),
        f2w2=ffn2_w2[0].astype(BF16), fnorm=row(final_norm))


def _rope_tables(pos):
    half = ROPE // 2
    inv = ROPE_THETA ** (-jnp.arange(half, dtype=F32) / half)
    ang = pos.astype(F32)[:, None] * inv[None, :]
    pad = lambda t: jnp.pad(jnp.concatenate([t, t], axis=1), ((0, 0), (0, LANE - ROPE)))
    return pad(jnp.cos(ang)), pad(jnp.sin(ang))


def _state_to_pairs(s):
    B = s.shape[0]
    s = s.reshape(B, N_PAIR, 2, HD, HD)
    z = jnp.zeros_like(s[:, :, 0])
    top = jnp.concatenate([s[:, :, 0], z], axis=-1)
    bot = jnp.concatenate([z, s[:, :, 1]], axis=-1)
    return jnp.concatenate([top, bot], axis=-2)


def _pairs_to_state(s):
    B = s.shape[0]
    return jnp.stack([s[:, :, :HD, :HD], s[:, :, HD:, HD:]], axis=2).reshape(B, HEADS, HD, HD)


def _pad_rows(t, n):
    return jnp.pad(t, ((0, 0), (0, n - t.shape[1]), (0, 0)))


def _tile(n, pref):
    return pref if n % pref == 0 else n


def kernel(x_prompt, x_sample, cache_ckv, cache_krope, state_wkv, state_shift, meta_tokens, norm_ffn1, ffn1_w1, ffn1_w3, ffn1_w2, norm_mix, w_in, mu_shift, w0, w_w2, a0, w_a2, w_g2, k_k, k_a, r_k, ln_x_w, ln_x_b, q_norm, w_q_up, kv_norm, w_uk, w_uv, w_out, norm_ffn2, ffn2_w1, ffn2_w3, ffn2_w2, final_norm):
    assert w_in.shape[0] == 1, "single-layer stack"
    B, S, _ = x_prompt.shape
    Bd, Tn, _ = x_sample.shape
    past = cache_ckv.shape[2]
    assert S % ATT_GROUP == 0 and Tn % SCAN == 0
    W = _prep_weights(norm_ffn1, ffn1_w1, ffn1_w3, ffn1_w2, norm_mix, w_in, mu_shift, w0, w_w2, a0, w_a2, w_g2,
                      k_k, k_a, r_k, ln_x_w, ln_x_b, q_norm, w_q_up, kv_norm, w_uk, w_uv, w_out, norm_ffn2,
                      ffn2_w1, ffn2_w3, ffn2_w2, final_norm)
    scan_keys = ("r", "lw", "k", "v", "a", "b")

    cos_m, sin_m = _rope_tables(jnp.arange(N_META))
    xm = meta_tokens[None].astype(F32)
    am = _stage_a(_ffn1(xm, W, N_META), jnp.zeros((1, 1, RWC), F32), cos_m, sin_m, W, N_META)
    _, st_m = _rwkv([_pad_rows(am[n], SCAN) for n in scan_keys], jnp.zeros((1, N_PAIR, PAIR, PAIR), F32))

    tm = _tile(S, 512)
    cos_p, sin_p = _rope_tables(N_META + jnp.arange(S))
    x1_p = _ffn1(x_prompt, W, _tile(B * S, 1024))
    ap = _stage_a(x1_p, jnp.broadcast_to(am["shift"], (B, 1, RWC)), cos_p, sin_p, W, tm)
    y_p, st_p = _rwkv([ap[n] for n in scan_keys], jnp.broadcast_to(st_m, (B, N_PAIR, PAIR, PAIR)))
    keys_m = _pad_rows(am["kcat"], LANE)[0]
    mla_p = _attention(ap["q"], ap["kcat"], keys_m, W["wuv"], ATT_GROUP, S, N_META)
    y_prompt = _stage_d(x1_p, y_p, ap["bon"], ap["g"], mla_p, W, tm)

    cos_s, sin_s = _rope_tables(N_META + past + jnp.arange(Tn))
    x1_s = _ffn1(x_sample, W, _tile(Bd * Tn, 512))
    as_ = _stage_a(x1_s, state_shift[0].astype(F32), cos_s, sin_s, W, Tn)
    y_s, st_s = _rwkv([as_[n] for n in scan_keys], _state_to_pairs(state_wkv[0].astype(F32)))
    cache_k = jnp.concatenate([cache_ckv[0], cache_krope[0], jnp.zeros((Bd, past, QK_W - KV_LORA - ROPE), F32)],
                              axis=-1).astype(BF16)
    n_valid = past + Tn + N_META
    n_keys = -(-n_valid // ATT_GROUP) * ATT_GROUP
    keys_s = jnp.concatenate([cache_k, as_["kcat"], jnp.broadcast_to(am["kcat"], (Bd, N_META, QK_W)),
                              jnp.zeros((Bd, n_keys - n_valid, QK_W), BF16)], axis=1)
    mla_s = _attention(as_["q"], keys_s, keys_m, W["wuv"], Tn, 0, n_valid)
    y_sample = _stage_d(x1_s, y_s, as_["bon"], as_["g"], mla_s, W, _tile(Bd * Tn, 512))

    bc = lambda t, n: jnp.broadcast_to(t, (n,) + t.shape[1:])
    ckv_p = jnp.concatenate([bc(am["c"], B), ap["c"]], axis=1)[None]
    kr_p = jnp.concatenate([bc(am["kr"], B), ap["kr"]], axis=1)[None, ..., :ROPE]
    return (y_prompt, y_sample, ckv_p, kr_p, _pairs_to_state(st_p)[None], ap["shift"][None],
            as_["c"][None], as_["kr"][None, ..., :ROPE], _pairs_to_state(st_s)[None], as_["shift"][None])
```

```python
import functools

import jax
import jax.numpy as jnp
from jax import lax
from jax.experimental import pallas as pl
from jax.experimental.pallas import tpu as pltpu

F32 = jnp.float32
BF16 = jnp.bfloat16

D_MODEL = 1024
D_FF = 2816
CHUNK = 64
N_META = 16
HEADS = 8
HD = 64
RW = HEADS * HD
ROPE = 32
Q_LORA = 256
KV_LORA = 128
NORM_EPS = 1e-6
GN_EPS = 64e-5
MLA_SCALE = (HD + ROPE) ** -0.5
ROPE_THETA = 10000.0

LANE = 128
SCAN = 64
PAIR = 2 * HD
N_PAIR = HEADS // 2
RWKV_ROWS = 4
QK_W = 256

O_R, O_K, O_V, O_WA, O_GL = 0, 512, 1024, 1536, 1664
RWC = 1792
O_Q, O_KV, O_KR = 1792, 2048, 2176
PW = 2304
ROT_SHIFT = LANE - ROPE
FF_CHUNK = 256
VMEM_LIMIT = 56 * 1024 * 1024
NEG = -1e30
LOG2E = 1.4426950408889634
ATT_ROWS = 1024
ATT_GROUP = 512


def _dot(a, b):
    return jnp.dot(a.astype(BF16), b.astype(BF16), preferred_element_type=F32)


def _dot_nt(a, b):
    return lax.dot_general(a.astype(BF16), b.astype(BF16), (((1,), (1,)), ((), ())),
                           preferred_element_type=F32)


def _split2(x):
    hi = x.astype(BF16)
    lo = (x - hi.astype(F32)).astype(BF16)
    return hi, lo


def _dot_x2(a, b):
    hi, lo = _split2(a)
    bb = b.astype(BF16)
    return (jnp.dot(hi, bb, preferred_element_type=F32) + jnp.dot(lo, bb, preferred_element_type=F32))


def _rms(x, g):
    return x * lax.rsqrt(jnp.mean(x * x, axis=-1, keepdims=True) + NORM_EPS) * g


def _ffn(h, w1_ref, w3_ref, w2_ref):
    acc = None
    for c in range(D_FF // FF_CHUNK):
        sl = slice(c * FF_CHUNK, (c + 1) * FF_CHUNK)
        a = jnp.dot(h, w1_ref[:, sl], preferred_element_type=F32)
        b = jnp.dot(h, w3_ref[:, sl], preferred_element_type=F32)
        z = (a * jax.nn.sigmoid(a) * b).astype(BF16)
        t = jnp.dot(z, w2_ref[sl, :], preferred_element_type=F32)
        acc = t if acc is None else acc + t
    return acc


def _ffn1_kernel(x_ref, n1_ref, w1_ref, w3_ref, w2_ref, o_ref):
    x = x_ref[...]
    h = _rms(x, n1_ref[...]).astype(BF16)
    o_ref[...] = x + 0.5 * _ffn(h, w1_ref, w3_ref, w2_ref)


def _full(shape):
    return pl.BlockSpec(memory_space=pltpu.VMEM)


def _ffn1(x, W, tm):
    B, T, _ = x.shape
    n = B * T
    tok = pl.BlockSpec((tm, D_MODEL), lambda i: (i, 0))
    weights = [W["n1"], W["f1w1"], W["f1w3"], W["f1w2"]]
    out = pl.pallas_call(
        _ffn1_kernel,
        grid=(n // tm,),
        in_specs=[tok] + [_full(w.shape) for w in weights],
        out_specs=tok,
        out_shape=jax.ShapeDtypeStruct((n, D_MODEL), F32),
        compiler_params=pltpu.CompilerParams(dimension_semantics=("arbitrary",), vmem_limit_bytes=VMEM_LIMIT),
        name="ffn1",
    )(x.reshape(n, D_MODEL), *weights)
    return out.reshape(B, T, D_MODEL)


def _rope(blk, cos, sin):
    return blk * cos + pltpu.roll(blk, ROT_SHIFT, 1) * sin


def _stage_a_kernel(x1_ref, sh0_ref, cos_ref, sin_ref,
                    nmix_ref, win_ref, mu_ref,
                    w0_ref, ww2_ref, a0_ref, wa2_ref, wg2_ref, kk_ref, ka_ref, rk_ref, ones_ref,
                    qn_ref, wqn_ref, wqr_ref, kvn_ref, wuk_ref,
                    r_o, lw_o, k_o, v_o, a_o, b_o, bon_o, g_o, c_o, kr_o, kcat_o, q_o, sh_o,
                    prev_s, *, tm, n_sub):
    t = pl.program_id(1)

    @pl.when(t == 0)
    def _():
        prev_s[...] = sh0_ref[0]

    ts = tm // n_sub

    def rows_block(rows, prev_row):
        hm = _rms(x1_ref[0, rows, :], nmix_ref[...]).astype(BF16)
        p = jnp.dot(hm, win_ref[...], preferred_element_type=F32)

        prw = p[:, :RWC]
        rolled = pltpu.roll(prw, 1, 0)
        row = lax.broadcasted_iota(jnp.int32, prw.shape, 0)
        prev = jnp.where(row == 0, prev_row, rolled)
        s = prw + mu_ref[...] * (prev - prw)

        r = s[:, O_R:O_R + RW]
        k = s[:, O_K:O_K + RW]
        v = s[:, O_V:O_V + RW]
        wa = s[:, O_WA:O_WA + LANE]
        gl = s[:, O_GL:O_GL + LANE]
        z = w0_ref[...] + _dot(jnp.tanh(wa), ww2_ref[...])
        softplus = jnp.maximum(-z, 0.0) + jnp.log(1.0 + jnp.exp(-jnp.abs(z)))
        logw = -softplus - 0.5
        a_sig = jax.nn.sigmoid(a0_ref[...] + _dot(wa, wa2_ref[...]))
        kk = k * kk_ref[...]
        ss = _dot(kk * kk, ones_ref[...])
        kk = kk / jnp.maximum(jnp.sqrt(ss), 1e-12)
        k2 = k * (1.0 + (a_sig - 1.0) * ka_ref[...])

        cos = cos_ref[rows, :]
        sin = sin_ref[rows, :]
        c = _rms(p[:, O_KV:O_KV + KV_LORA], kvn_ref[...])
        kr = _rope(p[:, O_KR:O_KR + LANE], cos, sin)
        cq = _rms(p[:, O_Q:O_Q + Q_LORA], qn_ref[...]).astype(BF16)
        qn = jnp.dot(cq, wqn_ref[...], preferred_element_type=F32)
        qlat = _dot(qn, wuk_ref[...])
        qr = jnp.dot(cq, wqr_ref[...], preferred_element_type=F32)
        q = []
        for hh in range(HEADS):
            sl = slice(hh * LANE, (hh + 1) * LANE)
            qrh = _rope(qr[:, sl], cos, sin)
            q.append((jnp.concatenate([qlat[:, sl], qrh], axis=1) * (MLA_SCALE * LOG2E)).astype(BF16))
        outs = dict(r=r, lw=-jnp.exp(logw), k=k2, v=v.astype(BF16), a=-kk, b=kk * a_sig,
                    bon=_dot(r * k2 * rk_ref[...], ones_ref[...]) * v, g=_dot(jax.nn.sigmoid(gl), wg2_ref[...]),
                    c=c, kr=kr[:, :ROPE], kcat=jnp.concatenate([c, kr], axis=1).astype(BF16), q=q)
        return outs, prw[ts - 1:ts, :]

    prev_row = prev_s[...]
    done = []
    for i in range(n_sub):
        rows = slice(i * ts, (i + 1) * ts)
        outs, prev_row = rows_block(rows, prev_row)
        done.append((rows, outs))
    prev_s[...] = prev_row
    sh_o[0] = prev_row
    refs = dict(r=r_o, lw=lw_o, k=k_o, v=v_o, a=a_o, b=b_o, bon=bon_o, g=g_o, c=c_o, kr=kr_o, kcat=kcat_o)
    for rows, outs in done:
        for name, ref in refs.items():
            ref[0, rows, :] = outs[name]
        for hh in range(HEADS):
            q_o[0, hh, rows, :] = outs["q"][hh]


def _stage_a(x1, sh0, cos, sin, W, tm):
    B, T, _ = x1.shape
    nt = T // tm
    tok = lambda w: pl.BlockSpec((1, tm, w), lambda b, t: (b, t, 0))
    weights = [W["nmix"], W["win"], W["mu"],
               W["w0"], W["ww2"], W["a0"], W["wa2"], W["wg2"], W["kk"], W["ka"], W["rk"], W["ones"],
               W["qn"], W["wqn"], W["wqr"], W["kvn"], W["wuk"]]
    in_specs = ([tok(D_MODEL), pl.BlockSpec((1, 1, RWC), lambda b, t: (b, 0, 0)),
                 pl.BlockSpec((tm, LANE), lambda b, t: (t, 0)), pl.BlockSpec((tm, LANE), lambda b, t: (t, 0))]
                + [_full(w.shape) for w in weights])
    f = lambda w, dt=F32: jax.ShapeDtypeStruct((B, T, w), dt)
    names = ["r", "lw", "k", "v", "a", "b", "bon", "g", "c", "kr", "kcat", "q", "shift"]
    out_shape = [f(RW), f(RW), f(RW), f(RW, BF16), f(RW), f(RW), f(RW), f(RW), f(KV_LORA), f(ROPE), f(QK_W, BF16),
                 jax.ShapeDtypeStruct((B, HEADS, T, QK_W), BF16), jax.ShapeDtypeStruct((B, 1, RWC), F32)]
    out_specs = ([tok(RW)] * 8 + [tok(KV_LORA), tok(ROPE), tok(QK_W),
                                  pl.BlockSpec((1, HEADS, tm, QK_W), lambda b, t: (b, 0, t, 0)),
                                  pl.BlockSpec((1, 1, RWC), lambda b, t: (b, 0, 0))])
    outs = pl.pallas_call(
        functools.partial(_stage_a_kernel, tm=tm, n_sub=2 if tm % 512 == 0 else 1),
        grid=(B, nt), in_specs=in_specs, out_specs=out_specs, out_shape=out_shape,
        scratch_shapes=[pltpu.VMEM((1, RWC), F32)],
        compiler_params=pltpu.CompilerParams(dimension_semantics=("arbitrary", "arbitrary"),
                                             vmem_limit_bytes=VMEM_LIMIT),
        name="stage_a",
    )(x1, sh0, cos, sin, *weights)
    return dict(zip(names, outs))


def _stack(x, lo_mask):
    return jnp.concatenate([jnp.where(lo_mask, x, 0.0), jnp.where(lo_mask, 0.0, x)], axis=0)


def _rwkv_kernel(r_ref, lw_ref, k_ref, v_ref, a_ref, b_ref, s0_ref, y_o, st_o, st_s, *, nb):
    c = pl.program_id(1)

    @pl.when(c == 0)
    def _():
        st_s[...] = s0_ref[...]

    n2 = 2 * SCAN
    ti = lax.broadcasted_iota(jnp.int32, (SCAN, SCAN), 0)
    tj = lax.broadcasted_iota(jnp.int32, (SCAN, SCAN), 1)
    tri = jnp.where(ti >= tj, 1.0, 0.0).astype(BF16)
    ri = lax.broadcasted_iota(jnp.int32, (n2, n2), 0) % SCAN
    ci = lax.broadcasted_iota(jnp.int32, (n2, n2), 1) % SCAN
    strict = ri > ci
    incl = ri >= ci
    lo_mask = lax.broadcasted_iota(jnp.int32, (SCAN, PAIR), 1) < HD

    chains = [(i, j) for i in range(nb) for j in range(N_PAIR)]
    sl = lambda j: slice(j * PAIR, (j + 1) * PAIR)
    sts = [st_s[i, j] for i, j in chains]
    ins = [[ref[i, :, sl(j)] for ref in (r_ref, k_ref, v_ref, a_ref, b_ref)] for i, j in chains]

    cums, lws = [], []
    for i in range(nb):
        lw_all = lw_ref[i]
        l1 = lw_all.astype(BF16)
        d1 = lw_all - l1.astype(F32)
        l2 = d1.astype(BF16)
        l3 = (d1 - l2.astype(F32)).astype(BF16)
        cum_all = (jnp.dot(tri, l1, preferred_element_type=F32) + jnp.dot(tri, l2, preferred_element_type=F32)
                   + jnp.dot(tri, l3, preferred_element_type=F32))
        for j in range(N_PAIR):
            cums.append(cum_all[:, sl(j)])
            lws.append(lw_all[:, sl(j)])

    def prep(n):
        r, k, v, a, b = ins[n]
        cum = cums[n]
        cum_c = cum[SCAN - 1:SCAN, :]
        e_n = jnp.exp(-cum)
        e_c = jnp.exp(cum_c - cum)
        return dict(
            p_c=jnp.exp(cum_c),
            ls=_stack(a * jnp.exp(cum - lws[n]), lo_mask).astype(BF16),
            rs=_stack(r * jnp.exp(cum), lo_mask),
            bs=_stack(b * e_n, lo_mask).astype(BF16), ks=_stack(k * e_n, lo_mask).astype(BF16),
            vs=_stack(v.astype(F32), lo_mask).astype(BF16),
            bh=_stack(b * e_c, lo_mask).astype(BF16), kh=_stack(k * e_c, lo_mask).astype(BF16))

    P = [prep(n) for n in range(len(chains))]
    aas = [_dot_nt(jnp.concatenate([p["ls"], p["rs"].astype(BF16)], axis=0),
                   jnp.concatenate([p["bs"], p["ks"]], axis=0)) for p in P]
    pws = [jnp.where(strict, aa[:n2, :n2], 0.0) for aa in aas]
    aks = [jnp.where(strict, aa[:n2, n2:], 0.0) for aa in aas]
    rbs = [jnp.where(incl, aa[n2:, :n2], 0.0) for aa in aas]
    rks = [jnp.where(incl, aa[n2:, n2:], 0.0) for aa in aas]

    xs = [jnp.concatenate([p["ls"].astype(F32), _dot(ak, p["vs"])], axis=1) for p, ak in zip(P, aks)]
    def live_rows(m, z):
        return m if z == 0 else jnp.concatenate([m[z:SCAN], m[SCAN + z:]], axis=0)

    def put_rows(full, part, z):
        if z == 0:
            return part
        h = SCAN - z
        return jnp.concatenate([full[:z], part[:h], full[SCAN:SCAN + z], part[h:]], axis=0)

    for it in range(6):
        z = (1 << it) if (1 << it) % 16 == 0 else 0
        xs = [put_rows(x, live_rows(x, z) + _dot(live_rows(pw, z), x), z) for x, pw in zip(xs, pws)]
        if it < 5:
            z2 = (2 << it) if (2 << it) % 16 == 0 else 0
            zero = jnp.zeros((n2, n2), F32)
            pws = [put_rows(zero, _dot(live_rows(pw, z2), pw), z2) for pw in pws]

    wls = [x[:, :PAIR] for x in xs]
    uvs = [jnp.concatenate([x[:, PAIR:], p["vs"].astype(F32)], axis=0) for x, p in zip(xs, P)]
    rqs = [p["rs"] + _dot(rb, wl) for p, rb, wl in zip(P, rbs, wls)]
    y0s = [_dot(jnp.concatenate([rb, rk], axis=1), uv) for rb, rk, uv in zip(rbs, rks, uvs)]
    gms = [_dot(wl.T, p["bh"]) for wl, p in zip(wls, P)]
    hms = [_dot(uv.T, jnp.concatenate([p["bh"], p["kh"]], axis=0)) for uv, p in zip(uvs, P)]
    yss = [_dot_nt(rq, st) + y0 for rq, st, y0 in zip(rqs, sts, y0s)]
    ys_out = [ys[:SCAN] + ys[SCAN:] for ys in yss]
    st_new = [p["p_c"] * st + _dot_x2(st, g_m) + h_m for p, st, g_m, h_m in zip(P, sts, gms, hms)]

    for i in range(nb):
        y_o[i] = jnp.concatenate(ys_out[i * N_PAIR:(i + 1) * N_PAIR], axis=1)
    for n, (i, j) in enumerate(chains):
        st_s[i, j] = st_new[n]
        st_o[i, j] = st_new[n]


def _rwkv(ins, s0):
    B, T, _ = ins[0].shape
    nc = T // SCAN
    nb = RWKV_ROWS if B % RWKV_ROWS == 0 else 1
    tok = pl.BlockSpec((nb, SCAN, RW), lambda b, c: (b, c, 0))
    st_spec = pl.BlockSpec((nb, N_PAIR, PAIR, PAIR), lambda b, c: (b, 0, 0, 0))
    return pl.pallas_call(
        functools.partial(_rwkv_kernel, nb=nb),
        grid=(B // nb, nc), in_specs=[tok] * 6 + [st_spec], out_specs=[tok, st_spec],
        out_shape=[jax.ShapeDtypeStruct((B, T, RW), F32), jax.ShapeDtypeStruct((B, N_PAIR, PAIR, PAIR), F32)],
        scratch_shapes=[pltpu.VMEM((nb, N_PAIR, PAIR, PAIR), F32)],
        compiler_params=pltpu.CompilerParams(dimension_semantics=("arbitrary", "arbitrary"),
                                             vmem_limit_bytes=VMEM_LIMIT),
        name="rwkv_scan",
    )(*ins, s0)


def _attn_kernel(q_ref, k_ref, kx_ref, wuv_ref, o_ref, m_s, l_s, acc_s, *, tq, n_main, n_valid_extra):
    i = pl.program_id(1)
    rows = HEADS * tq
    q = q_ref[0].reshape(rows, QK_W)
    def group(kb, fix, blk, first=False):
        vb = kb[:, :KV_LORA]
        reps = kb.shape[0] // LANE

        def chain(rb):
            s = lax.dot_general(q[rb], kb, (((1,), (1,)), ((), ())), preferred_element_type=F32)
            if fix is not None:
                s = fix(s)
            yield
            m_cur = jnp.max(s, axis=1, keepdims=True)
            if first:
                m_next = jnp.broadcast_to(m_cur, (blk, LANE))
            else:
                m_prev = m_s[rb, :]
                m_next = jnp.maximum(m_prev, m_cur)
                alpha = jnp.exp2(m_prev - m_next)
            p = jnp.exp2(s - jnp.concatenate([m_next] * reps, axis=1))
            l_new = jnp.broadcast_to(jnp.sum(p, axis=1, keepdims=True), (blk, LANE))
            if not first:
                l_new = alpha * l_s[rb, :] + l_new
            yield
            acc_new = jnp.dot(p.astype(BF16), vb, preferred_element_type=F32)
            if not first:
                acc_new = alpha * acc_s[rb, :] + acc_new
            return rb, m_next, l_new, acc_new

        pending = [slice(j * blk, (j + 1) * blk) for j in range(rows // blk)]
        running, done = [], []
        while pending or running:
            if pending:
                running.append(chain(pending.pop(0)))
            for gen in list(running):
                try:
                    next(gen)
                except StopIteration as stop:
                    done.append(stop.value)
                    running.remove(gen)
        for rb, m_next, l_new, acc_new in done:
            m_s[rb, :] = m_next
            l_s[rb, :] = l_new
            acc_s[rb, :] = acc_new

    if n_main:
        def body(g, carry):
            group(k_ref[0, pl.ds(pl.multiple_of(g * tq, tq), tq), :], None, min(rows, ATT_ROWS))
            return carry

        k_diag = k_ref[0, pl.ds(pl.multiple_of(i * tq, tq), tq), :]
        k_extra = kx_ref[...]
        rc = lax.broadcasted_iota(jnp.int32, (tq, tq), 0) // CHUNK
        cc = lax.broadcasted_iota(jnp.int32, (tq, tq), 1) // CHUNK
        diag_ok = cc <= rc
        extra_ok = lax.broadcasted_iota(jnp.int32, (tq, LANE), 1) < n_valid_extra

        def fix(s):
            return jnp.concatenate([jnp.where(diag_ok, s[:, :tq], NEG), jnp.where(extra_ok, s[:, tq:], NEG)], axis=1)

        group(jnp.concatenate([k_diag, k_extra], axis=0), fix, tq, first=True)
        lax.fori_loop(0, i, body, 0)
    else:
        blk = min(rows, ATT_ROWS)
        n_groups = k_ref.shape[1] // ATT_GROUP
        for g in range(n_groups):
            valid = min(ATT_GROUP, n_valid_extra - g * ATT_GROUP)
            fix = None
            if valid < ATT_GROUP:
                ok = lax.broadcasted_iota(jnp.int32, (blk, ATT_GROUP), 1) < valid
                fix = lambda s, ok=ok: jnp.where(ok, s, NEG)
            group(k_ref[0, g * ATT_GROUP:(g + 1) * ATT_GROUP, :], fix, blk, first=g == 0)

    lat = acc_s[...] / l_s[...]
    lat_wide = jnp.concatenate([lat[hh * tq:(hh + 1) * tq] for hh in range(HEADS)], axis=1).astype(BF16)
    o_ref[0] = jnp.dot(lat_wide, wuv_ref[...], preferred_element_type=F32)


def _attention(q, keys, keys_extra, wuv, tq, n_main, n_valid_extra):
    B, _, T, _ = q.shape
    tkeys = keys.shape[1]
    kern = functools.partial(_attn_kernel, tq=tq, n_main=n_main, n_valid_extra=n_valid_extra)
    rows = HEADS * tq
    return pl.pallas_call(
        kern,
        grid=(B, T // tq),
        in_specs=[pl.BlockSpec((1, HEADS, tq, QK_W), lambda b, i: (b, 0, i, 0)),
                  pl.BlockSpec((1, tkeys, QK_W), lambda b, i: (b, 0, 0)),
                  _full(keys_extra.shape), _full(wuv.shape)],
        out_specs=pl.BlockSpec((1, tq, RW), lambda b, i: (b, i, 0)),
        out_shape=jax.ShapeDtypeStruct((B, T, RW), F32),
        scratch_shapes=[pltpu.VMEM((rows, LANE), F32)] * 3,
        compiler_params=pltpu.CompilerParams(dimension_semantics=("arbitrary", "arbitrary"),
                                             vmem_limit_bytes=VMEM_LIMIT),
        name="mla_attention",
    )(q, keys, keys_extra, wuv)


def _stage_d_kernel(x1_ref, y_ref, bon_ref, g_ref, mla_ref, ones_ref, lnw_ref, lnb_ref, wo_ref,
                    n2_ref, w1_ref, w3_ref, w2_ref, fn_ref, o_ref):
    y = y_ref[...]
    mu = _dot_x2(y, ones_ref[...]) * (1.0 / HD)
    d = y - mu
    var = _dot(d * d, ones_ref[...]) * (1.0 / HD)
    yn = d * lax.rsqrt(var + GN_EPS) * lnw_ref[...] + lnb_ref[...]
    rw = ((yn + bon_ref[...]) * g_ref[...]).astype(BF16)
    x2 = (x1_ref[...] + jnp.dot(rw, wo_ref[:RW, :], preferred_element_type=F32)
          + jnp.dot(mla_ref[...].astype(BF16), wo_ref[RW:, :], preferred_element_type=F32))
    h = _rms(x2, n2_ref[...]).astype(BF16)
    x3 = x2 + 0.5 * _ffn(h, w1_ref, w3_ref, w2_ref)
    o_ref[...] = _rms(x3, fn_ref[...])


def _stage_d(x1, y, bon, g, mla, W, tm):
    B, T, _ = x1.shape
    n = B * T
    flat = lambda t: t.reshape(n, t.shape[-1])
    tok = lambda w: pl.BlockSpec((tm, w), lambda i: (i, 0))
    weights = [W["ones"], W["lnw"], W["lnb"], W["wout"], W["n2"], W["f2w1"], W["f2w3"], W["f2w2"], W["fnorm"]]
    out = pl.pallas_call(
        _stage_d_kernel,
        grid=(n // tm,),
        in_specs=[tok(D_MODEL), tok(RW), tok(RW), tok(RW), tok(RW)] + [_full(w.shape) for w in weights],
        out_specs=tok(D_MODEL),
        out_shape=jax.ShapeDtypeStruct((n, D_MODEL), F32),
        compiler_params=pltpu.CompilerParams(dimension_semantics=("arbitrary",), vmem_limit_bytes=VMEM_LIMIT),
        name="stage_d",
    )(flat(x1), flat(y), flat(bon), flat(g), flat(mla), *weights)
    return out.reshape(B, T, D_MODEL)


def _rot_cols(w):
    half = ROPE // 2
    return jnp.concatenate([-w[..., half:], w[..., :half]], axis=-1)


def _prep_weights(norm_ffn1, ffn1_w1, ffn1_w3, ffn1_w2, norm_mix, w_in, mu_shift, w0, w_w2, a0, w_a2, w_g2, k_k,
                  k_a, r_k, ln_x_w, ln_x_b, q_norm, w_q_up, kv_norm, w_uk, w_uv, w_out, norm_ffn2, ffn2_w1,
                  ffn2_w3, ffn2_w2, final_norm):
    row = lambda t: t.reshape(1, -1).astype(F32)
    wi = w_in[0]
    wkr = wi[:, O_KR:O_KR + ROPE]
    win = jnp.concatenate([wi, _rot_cols(wkr), jnp.zeros((D_MODEL, LANE - 2 * ROPE), F32)], axis=1)
    zl = jnp.zeros((HD, RW), F32)
    wq = w_q_up[0]
    wq_rope = wq[:, :, HD:]
    wqr = jnp.concatenate([wq_rope, _rot_cols(wq_rope), jnp.zeros((Q_LORA, HEADS, LANE - 2 * ROPE), F32)], axis=-1)
    eye = jnp.eye(HEADS, dtype=F32)
    wuk_bd = jnp.einsum("chd,hg->hdgc", w_uk[0], eye).reshape(HEADS * HD, HEADS * KV_LORA)
    wuv_bd = jnp.einsum("chd,hg->hcgd", w_uv[0], eye).reshape(HEADS * KV_LORA, RW)
    ones_bd = jnp.kron(eye, jnp.ones((HD, HD), F32))
    return dict(
        n1=row(norm_ffn1), f1w1=ffn1_w1[0].astype(BF16), f1w3=ffn1_w3[0].astype(BF16), f1w2=ffn1_w2[0].astype(BF16),
        nmix=row(norm_mix), win=win.astype(BF16), mu=row(mu_shift),
        w0=row(w0), ww2=jnp.concatenate([w_w2[0], zl], axis=0).astype(BF16), a0=row(a0),
        wa2=jnp.concatenate([zl, w_a2[0]], axis=0).astype(BF16), wg2=w_g2[0].astype(BF16),
        kk=row(k_k), ka=row(k_a), rk=row(r_k), ones=ones_bd.astype(BF16),
        qn=row(q_norm), wqn=wq[:, :, :HD].reshape(Q_LORA, RW).astype(BF16),
        wqr=wqr.reshape(Q_LORA, HEADS * LANE).astype(BF16),
        kvn=row(kv_norm), wuk=wuk_bd.astype(BF16), wuv=wuv_bd.astype(BF16),
        lnw=row(ln_x_w), lnb=row(ln_x_b), wout=w_out[0].astype(BF16),
        n2=row(norm_ffn2), f2w1=ffn2_w1[0].astype(BF16), f2w3=ffn2_w3[0].astype(BF16),
        f2w2=ffn2_w2[0].astype(BF16), fnorm=row(final_norm))


def _rope_tables(pos):
    half = ROPE // 2
    inv = ROPE_THETA ** (-jnp.arange(half, dtype=F32) / half)
    ang = pos.astype(F32)[:, None] * inv[None, :]
    pad = lambda t: jnp.pad(jnp.concatenate([t, t], axis=1), ((0, 0), (0, LANE - ROPE)))
    return pad(jnp.cos(ang)), pad(jnp.sin(ang))


def _state_to_pairs(s):
    B = s.shape[0]
    s = s.reshape(B, N_PAIR, 2, HD, HD)
    z = jnp.zeros_like(s[:, :, 0])
    top = jnp.concatenate([s[:, :, 0], z], axis=-1)
    bot = jnp.concatenate([z, s[:, :, 1]], axis=-1)
    return jnp.concatenate([top, bot], axis=-2)


def _pairs_to_state(s):
    B = s.shape[0]
    return jnp.stack([s[:, :, :HD, :HD], s[:, :, HD:, HD:]], axis=2).reshape(B, HEADS, HD, HD)


def _pad_rows(t, n):
    return jnp.pad(t, ((0, 0), (0, n - t.shape[1]), (0, 0)))


def _tile(n, pref):
    return pref if n % pref == 0 else n


def kernel(x_prompt, x_sample, cache_ckv, cache_krope, state_wkv, state_shift, meta_tokens, norm_ffn1, ffn1_w1, ffn1_w3, ffn1_w2, norm_mix, w_in, mu_shift, w0, w_w2, a0, w_a2, w_g2, k_k, k_a, r_k, ln_x_w, ln_x_b, q_norm, w_q_up, kv_norm, w_uk, w_uv, w_out, norm_ffn2, ffn2_w1, ffn2_w3, ffn2_w2, final_norm):
    assert w_in.shape[0] == 1, "single-layer stack"
    B, S, _ = x_prompt.shape
    Bd, Tn, _ = x_sample.shape
    past = cache_ckv.shape[2]
    assert S % ATT_GROUP == 0 and Tn % SCAN == 0
    W = _prep_weights(norm_ffn1, ffn1_w1, ffn1_w3, ffn1_w2, norm_mix, w_in, mu_shift, w0, w_w2, a0, w_a2, w_g2,
                      k_k, k_a, r_k, ln_x_w, ln_x_b, q_norm, w_q_up, kv_norm, w_uk, w_uv, w_out, norm_ffn2,
                      ffn2_w1, ffn2_w3, ffn2_w2, final_norm)
    scan_keys = ("r", "lw", "k", "v", "a", "b")

    cos_m, sin_m = _rope_tables(jnp.arange(N_META))
    xm = meta_tokens[None].astype(F32)
    am = _stage_a(_ffn1(xm, W, N_META), jnp.zeros((1, 1, RWC), F32), cos_m, sin_m, W, N_META)
    _, st_m = _rwkv([_pad_rows(am[n], SCAN) for n in scan_keys], jnp.zeros((1, N_PAIR, PAIR, PAIR), F32))

    tm = _tile(S, 512)
    cos_p, sin_p = _rope_tables(N_META + jnp.arange(S))
    x1_p = _ffn1(x_prompt, W, _tile(B * S, 1024))
    ap = _stage_a(x1_p, jnp.broadcast_to(am["shift"], (B, 1, RWC)), cos_p, sin_p, W, tm)
    y_p, st_p = _rwkv([ap[n] for n in scan_keys], jnp.broadcast_to(st_m, (B, N_PAIR, PAIR, PAIR)))
    keys_m = _pad_rows(am["kcat"], LANE)[0]
    mla_p = _attention(ap["q"], ap["kcat"], keys_m, W["wuv"], ATT_GROUP, S, N_META)
    y_prompt = _stage_d(x1_p, y_p, ap["bon"], ap["g"], mla_p, W, tm)

    cos_s, sin_s = _rope_tables(N_META + past + jnp.arange(Tn))
    x1_s = _ffn1(x_sample, W, _tile(Bd * Tn, 512))
    as_ = _stage_a(x1_s, state_shift[0].astype(F32), cos_s, sin_s, W, Tn)
    y_s, st_s = _rwkv([as_[n] for n in scan_keys], _state_to_pairs(state_wkv[0].astype(F32)))
    cache_k = jnp.concatenate([cache_ckv[0], cache_krope[0], jnp.zeros((Bd, past, QK_W - KV_LORA - ROPE), F32)],
                              axis=-1).astype(BF16)
    n_valid = past + Tn + N_META
    n_keys = -(-n_valid // ATT_GROUP) * ATT_GROUP
    keys_s = jnp.concatenate([cache_k, as_["kcat"], jnp.broadcast_to(am["kcat"], (Bd, N_META, QK_W)),
                              jnp.zeros((Bd, n_keys - n_valid, QK_W), BF16)], axis=1)
    mla_s = _attention(as_["q"], keys_s, keys_m, W["wuv"], Tn, 0, n_valid)
    y_sample = _stage_d(x1_s, y_s, as_["bon"], as_["g"], mla_s, W, _tile(Bd * Tn, 512))

    bc = lambda t, n: jnp.broadcast_to(t, (n,) + t.shape[1:])
    ckv_p = jnp.concatenate([bc(am["c"], B), ap["c"]], axis=1)[None]
    kr_p = jnp.concatenate([bc(am["kr"], B), ap["kr"]], axis=1)[None]
    return (y_prompt, y_sample, ckv_p, kr_p, _pairs_to_state(st_p)[None], ap["shift"][None],
            as_["c"][None], as_["kr"][None], _pairs_to_state(st_s)[None], as_["shift"][None])
```

```python
import functools

import jax
import jax.numpy as jnp
from jax import lax
from jax.experimental import pallas as pl
from jax.experimental.pallas import tpu as pltpu

F32 = jnp.float32
BF16 = jnp.bfloat16

D_MODEL = 1024
D_FF = 2816
CHUNK = 64
N_META = 16
HEADS = 8
HD = 64
RW = HEADS * HD
ROPE = 32
Q_LORA = 256
KV_LORA = 128
NORM_EPS = 1e-6
GN_EPS = 64e-5
MLA_SCALE = (HD + ROPE) ** -0.5
ROPE_THETA = 10000.0

LANE = 128
SCAN = 64
PAIR = 2 * HD
N_PAIR = HEADS // 2
RWKV_ROWS = 4
QK_W = 256

O_R, O_K, O_V, O_WA, O_GL = 0, 512, 1024, 1536, 1664
RWC = 1792
O_Q, O_KV, O_KR = 1792, 2048, 2176
PW = 2304
ROT_SHIFT = LANE - ROPE
FF_CHUNK = 256
VMEM_LIMIT = 56 * 1024 * 1024
NEG = -1e30
LOG2E = 1.4426950408889634
ATT_ROWS = 1024
ATT_GROUP = 512


def _dot(a, b):
    return jnp.dot(a.astype(BF16), b.astype(BF16), preferred_element_type=F32)


def _dot_nt(a, b):
    return lax.dot_general(a.astype(BF16), b.astype(BF16), (((1,), (1,)), ((), ())),
                           preferred_element_type=F32)


def _split2(x):
    hi = x.astype(BF16)
    lo = (x - hi.astype(F32)).astype(BF16)
    return hi, lo


def _dot_x2(a, b):
    hi, lo = _split2(a)
    bb = b.astype(BF16)
    return (jnp.dot(hi, bb, preferred_element_type=F32) + jnp.dot(lo, bb, preferred_element_type=F32))


def _rms(x, g):
    return x * lax.rsqrt(jnp.mean(x * x, axis=-1, keepdims=True) + NORM_EPS) * g


def _ffn(h, w1_ref, w3_ref, w2_ref):
    acc = None
    for c in range(D_FF // FF_CHUNK):
        sl = slice(c * FF_CHUNK, (c + 1) * FF_CHUNK)
        a = jnp.dot(h, w1_ref[:, sl], preferred_element_type=F32)
        b = jnp.dot(h, w3_ref[:, sl], preferred_element_type=F32)
        z = (a * jax.nn.sigmoid(a) * b).astype(BF16)
        t = jnp.dot(z, w2_ref[sl, :], preferred_element_type=F32)
        acc = t if acc is None else acc + t
    return acc


def _ffn1_kernel(x_ref, n1_ref, w1_ref, w3_ref, w2_ref, o_ref):
    x = x_ref[...]
    h = _rms(x, n1_ref[...]).astype(BF16)
    o_ref[...] = x + 0.5 * _ffn(h, w1_ref, w3_ref, w2_ref)


def _full(shape):
    return pl.BlockSpec(memory_space=pltpu.VMEM)


def _ffn1(x, W, tm):
    B, T, _ = x.shape
    n = B * T
    tok = pl.BlockSpec((tm, D_MODEL), lambda i: (i, 0))
    weights = [W["n1"], W["f1w1"], W["f1w3"], W["f1w2"]]
    out = pl.pallas_call(
        _ffn1_kernel,
        grid=(n // tm,),
        in_specs=[tok] + [_full(w.shape) for w in weights],
        out_specs=tok,
        out_shape=jax.ShapeDtypeStruct((n, D_MODEL), F32),
        compiler_params=pltpu.CompilerParams(dimension_semantics=("arbitrary",), vmem_limit_bytes=VMEM_LIMIT),
        name="ffn1",
    )(x.reshape(n, D_MODEL), *weights)
    return out.reshape(B, T, D_MODEL)


def _rope(blk, cos, sin):
    return blk * cos + pltpu.roll(blk, ROT_SHIFT, 1) * sin


def _stage_a_kernel(x1_ref, sh0_ref, cos_ref, sin_ref,
                    nmix_ref, win_ref, mu_ref,
                    w0_ref, ww2_ref, a0_ref, wa2_ref, wg2_ref, kk_ref, ka_ref, rk_ref, ones_ref,
                    qn_ref, wqn_ref, wqr_ref, kvn_ref, wuk_ref,
                    r_o, lw_o, k_o, v_o, a_o, b_o, bon_o, g_o, c_o, kr_o, kcat_o, q_o, sh_o,
                    prev_s, *, tm, n_sub):
    t = pl.program_id(1)

    @pl.when(t == 0)
    def _():
        prev_s[...] = sh0_ref[0]

    ts = tm // n_sub

    def rows_block(rows, prev_row):
        hm = _rms(x1_ref[0, rows, :], nmix_ref[...]).astype(BF16)
        p = jnp.dot(hm, win_ref[...], preferred_element_type=F32)

        prw = p[:, :RWC]
        rolled = pltpu.roll(prw, 1, 0)
        row = lax.broadcasted_iota(jnp.int32, (8, RWC), 0)
        prev = jnp.concatenate([jnp.where(row == 0, prev_row, rolled[:8]), rolled[8:]], axis=0)
        s = prw + mu_ref[...] * (prev - prw)

        r = s[:, O_R:O_R + RW]
        k = s[:, O_K:O_K + RW]
        v = s[:, O_V:O_V + RW]
        wa = s[:, O_WA:O_WA + LANE]
        gl = s[:, O_GL:O_GL + LANE]
        z = w0_ref[...] + _dot(jnp.tanh(wa), ww2_ref[...])
        softplus = jnp.maximum(-z, 0.0) + jnp.log(1.0 + jnp.exp(-jnp.abs(z)))
        logw = -softplus - 0.5
        a_sig = jax.nn.sigmoid(a0_ref[...] + _dot(wa, wa2_ref[...]))
        kk = k * kk_ref[...]
        ss = _dot(kk * kk, ones_ref[...])
        kk = kk * lax.rsqrt(jnp.maximum(ss, 1e-24))
        k2 = k * (1.0 + (a_sig - 1.0) * ka_ref[...])

        cos = cos_ref[rows, :]
        sin = sin_ref[rows, :]
        c = _rms(p[:, O_KV:O_KV + KV_LORA], kvn_ref[...])
        kr = _rope(p[:, O_KR:O_KR + LANE], cos, sin)
        cq = _rms(p[:, O_Q:O_Q + Q_LORA], qn_ref[...]).astype(BF16)
        qn = jnp.dot(cq, wqn_ref[...], preferred_element_type=F32)
        qlat = _dot(qn, wuk_ref[...])
        qr = jnp.dot(cq, wqr_ref[...], preferred_element_type=F32)
        q = []
        for hh in range(HEADS):
            sl = slice(hh * LANE, (hh + 1) * LANE)
            qrh = _rope(qr[:, sl], cos, sin)
            q.append((jnp.concatenate([qlat[:, sl], qrh], axis=1) * (MLA_SCALE * LOG2E)).astype(BF16))
        outs = dict(r=r, lw=-jnp.exp(logw), k=k2, v=v.astype(BF16), a=-kk, b=kk * a_sig,
                    bon=_dot(r * k2 * rk_ref[...], ones_ref[...]) * v, g=_dot(jax.nn.sigmoid(gl), wg2_ref[...]),
                    c=c, kr=kr[:, :ROPE], kcat=jnp.concatenate([c, kr], axis=1).astype(BF16), q=q)
        return outs, prw[ts - 1:ts, :]

    prev_row = prev_s[...]
    done = []
    for i in range(n_sub):
        rows = slice(i * ts, (i + 1) * ts)
        outs, prev_row = rows_block(rows, prev_row)
        done.append((rows, outs))
    prev_s[...] = prev_row
    sh_o[0] = prev_row
    refs = dict(r=r_o, lw=lw_o, k=k_o, v=v_o, a=a_o, b=b_o, bon=bon_o, g=g_o, c=c_o, kr=kr_o, kcat=kcat_o)
    for rows, outs in done:
        for name, ref in refs.items():
            ref[0, rows, :] = outs[name]
        for hh in range(HEADS):
            q_o[0, hh, rows, :] = outs["q"][hh]


def _stage_a(x1, sh0, cos, sin, W, tm):
    B, T, _ = x1.shape
    nt = T // tm
    tok = lambda w: pl.BlockSpec((1, tm, w), lambda b, t: (b, t, 0))
    weights = [W["nmix"], W["win"], W["mu"],
               W["w0"], W["ww2"], W["a0"], W["wa2"], W["wg2"], W["kk"], W["ka"], W["rk"], W["ones"],
               W["qn"], W["wqn"], W["wqr"], W["kvn"], W["wuk"]]
    in_specs = ([tok(D_MODEL), pl.BlockSpec((1, 1, RWC), lambda b, t: (b, 0, 0)),
                 pl.BlockSpec((tm, LANE), lambda b, t: (t, 0)), pl.BlockSpec((tm, LANE), lambda b, t: (t, 0))]
                + [_full(w.shape) for w in weights])
    f = lambda w, dt=F32: jax.ShapeDtypeStruct((B, T, w), dt)
    names = ["r", "lw", "k", "v", "a", "b", "bon", "g", "c", "kr", "kcat", "q", "shift"]
    out_shape = [f(RW), f(RW), f(RW), f(RW, BF16), f(RW), f(RW), f(RW), f(RW), f(KV_LORA), f(ROPE), f(QK_W, BF16),
                 jax.ShapeDtypeStruct((B, HEADS, T, QK_W), BF16), jax.ShapeDtypeStruct((B, 1, RWC), F32)]
    out_specs = ([tok(RW)] * 8 + [tok(KV_LORA), tok(ROPE), tok(QK_W),
                                  pl.BlockSpec((1, HEADS, tm, QK_W), lambda b, t: (b, 0, t, 0)),
                                  pl.BlockSpec((1, 1, RWC), lambda b, t: (b, 0, 0))])
    outs = pl.pallas_call(
        functools.partial(_stage_a_kernel, tm=tm, n_sub=2 if tm % 512 == 0 else 1),
        grid=(B, nt), in_specs=in_specs, out_specs=out_specs, out_shape=out_shape,
        scratch_shapes=[pltpu.VMEM((1, RWC), F32)],
        compiler_params=pltpu.CompilerParams(dimension_semantics=("arbitrary", "arbitrary"),
                                             vmem_limit_bytes=VMEM_LIMIT),
        name="stage_a",
    )(x1, sh0, cos, sin, *weights)
    return dict(zip(names, outs))


def _stack(x, lo_mask):
    return jnp.concatenate([jnp.where(lo_mask, x, 0.0), jnp.where(lo_mask, 0.0, x)], axis=0)


def _rwkv_kernel(r_ref, lw_ref, k_ref, v_ref, a_ref, b_ref, s0_ref, y_o, st_o, st_s, *, nb):
    c = pl.program_id(1)

    @pl.when(c == 0)
    def _():
        st_s[...] = s0_ref[...]

    n2 = 2 * SCAN
    ti = lax.broadcasted_iota(jnp.int32, (SCAN, SCAN), 0)
    tj = lax.broadcasted_iota(jnp.int32, (SCAN, SCAN), 1)
    tri = jnp.where(ti >= tj, 1.0, 0.0).astype(BF16)
    ri = lax.broadcasted_iota(jnp.int32, (n2, n2), 0) % SCAN
    ci = lax.broadcasted_iota(jnp.int32, (n2, n2), 1) % SCAN
    strict = ri > ci
    incl = ri >= ci
    lo_mask = lax.broadcasted_iota(jnp.int32, (SCAN, PAIR), 1) < HD

    chains = [(i, j) for i in range(nb) for j in range(N_PAIR)]
    sl = lambda j: slice(j * PAIR, (j + 1) * PAIR)
    sts = [st_s[i, j] for i, j in chains]
    ins = [[ref[i, :, sl(j)] for ref in (r_ref, k_ref, v_ref, a_ref, b_ref)] for i, j in chains]

    cums, lws = [], []
    for i in range(nb):
        lw_all = lw_ref[i]
        l1 = lw_all.astype(BF16)
        d1 = lw_all - l1.astype(F32)
        l2 = d1.astype(BF16)
        l3 = (d1 - l2.astype(F32)).astype(BF16)
        cum_all = (jnp.dot(tri, l1, preferred_element_type=F32) + jnp.dot(tri, l2, preferred_element_type=F32)
                   + jnp.dot(tri, l3, preferred_element_type=F32))
        for j in range(N_PAIR):
            cums.append(cum_all[:, sl(j)])
            lws.append(lw_all[:, sl(j)])

    def prep(n):
        r, k, v, a, b = ins[n]
        cum = cums[n]
        cum_c = cum[SCAN - 1:SCAN, :]
        e_n = jnp.exp(-cum)
        e_c = jnp.exp(cum_c - cum)
        return dict(
            p_c=jnp.exp(cum_c),
            ls=_stack(a * jnp.exp(cum - lws[n]), lo_mask).astype(BF16),
            rs=_stack(r * jnp.exp(cum), lo_mask),
            bs=_stack(b * e_n, lo_mask).astype(BF16), ks=_stack(k * e_n, lo_mask).astype(BF16),
            vs=_stack(v.astype(F32), lo_mask).astype(BF16),
            bh=_stack(b * e_c, lo_mask).astype(BF16), kh=_stack(k * e_c, lo_mask).astype(BF16))

    P = [prep(n) for n in range(len(chains))]
    aas = [_dot_nt(jnp.concatenate([p["ls"], p["rs"].astype(BF16)], axis=0),
                   jnp.concatenate([p["bs"], p["ks"]], axis=0)) for p in P]
    pws = [jnp.where(strict, aa[:n2, :n2], 0.0) for aa in aas]
    aks = [jnp.where(strict, aa[:n2, n2:], 0.0) for aa in aas]
    rbs = [jnp.where(incl, aa[n2:, :n2], 0.0) for aa in aas]
    rks = [jnp.where(incl, aa[n2:, n2:], 0.0) for aa in aas]

    xs = [jnp.concatenate([p["ls"].astype(F32), _dot(ak, p["vs"])], axis=1) for p, ak in zip(P, aks)]
    def live_rows(m, z):
        return m if z == 0 else jnp.concatenate([m[z:SCAN], m[SCAN + z:]], axis=0)

    def put_rows(full, part, z):
        if z == 0:
            return part
        h = SCAN - z
        return jnp.concatenate([full[:z], part[:h], full[SCAN:SCAN + z], part[h:]], axis=0)

    for it in range(6):
        z = (1 << it) if (1 << it) % 16 == 0 else 0
        xs = [put_rows(x, live_rows(x, z) + _dot(live_rows(pw, z), x), z) for x, pw in zip(xs, pws)]
        if it < 5:
            z2 = (2 << it) if (2 << it) % 16 == 0 else 0
            zero = jnp.zeros((n2, n2), F32)
            pws = [put_rows(zero, _dot(live_rows(pw, z2), pw), z2) for pw in pws]

    wls = [x[:, :PAIR] for x in xs]
    uvs = [jnp.concatenate([x[:, PAIR:], p["vs"].astype(F32)], axis=0) for x, p in zip(xs, P)]
    rqs = [p["rs"] + _dot(rb, wl) for p, rb, wl in zip(P, rbs, wls)]
    y0s = [_dot(jnp.concatenate([rb, rk], axis=1), uv) for rb, rk, uv in zip(rbs, rks, uvs)]
    gms = [_dot(wl.T, p["bh"]) for wl, p in zip(wls, P)]
    hms = [_dot(uv.T, jnp.concatenate([p["bh"], p["kh"]], axis=0)) for uv, p in zip(uvs, P)]
    yss = [_dot_nt(rq, st) + y0 for rq, st, y0 in zip(rqs, sts, y0s)]
    ys_out = [ys[:SCAN] + ys[SCAN:] for ys in yss]
    st_new = [p["p_c"] * st + _dot(st, g_m) + h_m for p, st, g_m, h_m in zip(P, sts, gms, hms)]

    for i in range(nb):
        y_o[i] = jnp.concatenate(ys_out[i * N_PAIR:(i + 1) * N_PAIR], axis=1)
    for n, (i, j) in enumerate(chains):
        st_s[i, j] = st_new[n]
        st_o[i, j] = st_new[n]


def _rwkv(ins, s0):
    B, T, _ = ins[0].shape
    nc = T // SCAN
    nb = RWKV_ROWS if B % RWKV_ROWS == 0 else 1
    tok = pl.BlockSpec((nb, SCAN, RW), lambda b, c: (b, c, 0))
    st_spec = pl.BlockSpec((nb, N_PAIR, PAIR, PAIR), lambda b, c: (b, 0, 0, 0))
    return pl.pallas_call(
        functools.partial(_rwkv_kernel, nb=nb),
        grid=(B // nb, nc), in_specs=[tok] * 6 + [st_spec], out_specs=[tok, st_spec],
        out_shape=[jax.ShapeDtypeStruct((B, T, RW), F32), jax.ShapeDtypeStruct((B, N_PAIR, PAIR, PAIR), F32)],
        scratch_shapes=[pltpu.VMEM((nb, N_PAIR, PAIR, PAIR), F32)],
        compiler_params=pltpu.CompilerParams(dimension_semantics=("arbitrary", "arbitrary"),
                                             vmem_limit_bytes=VMEM_LIMIT),
        name="rwkv_scan",
    )(*ins, s0)


def _attn_kernel(q_ref, k_ref, kx_ref, wuv_ref, o_ref, m_s, l_s, acc_s, *, tq, n_main, n_valid_extra):
    i = pl.program_id(1)
    rows = HEADS * tq
    q = q_ref[0].reshape(rows, QK_W)
    def group(kb, fix, blk, first=False):
        vb = kb[:, :KV_LORA]
        reps = kb.shape[0] // LANE

        def chain(rb):
            s = lax.dot_general(q[rb], kb, (((1,), (1,)), ((), ())), preferred_element_type=F32)
            if fix is not None:
                s = fix(s)
            yield
            m_cur = jnp.max(s, axis=1, keepdims=True)
            if first:
                m_next = jnp.broadcast_to(m_cur, (blk, LANE))
            else:
                m_prev = m_s[rb, :]
                m_next = jnp.maximum(m_prev, m_cur)
                alpha = jnp.exp2(m_prev - m_next)
            p = jnp.exp2(s - jnp.concatenate([m_next] * reps, axis=1))
            l_new = jnp.broadcast_to(jnp.sum(p, axis=1, keepdims=True), (blk, LANE))
            if not first:
                l_new = alpha * l_s[rb, :] + l_new
            yield
            acc_new = jnp.dot(p.astype(BF16), vb, preferred_element_type=F32)
            if not first:
                acc_new = alpha * acc_s[rb, :] + acc_new
            return rb, m_next, l_new, acc_new

        pending = [slice(j * blk, (j + 1) * blk) for j in range(rows // blk)]
        running, done = [], []
        while pending or running:
            if pending:
                running.append(chain(pending.pop(0)))
            for gen in list(running):
                try:
                    next(gen)
                except StopIteration as stop:
                    done.append(stop.value)
                    running.remove(gen)
        for rb, m_next, l_new, acc_new in done:
            m_s[rb, :] = m_next
            l_s[rb, :] = l_new
            acc_s[rb, :] = acc_new

    if n_main:
        def body(g, carry):
            group(k_ref[0, pl.ds(pl.multiple_of(g * tq, tq), tq), :], None, min(rows, ATT_ROWS))
            return carry

        k_diag = k_ref[0, pl.ds(pl.multiple_of(i * tq, tq), tq), :]
        k_extra = kx_ref[...]
        rc = lax.broadcasted_iota(jnp.int32, (tq, tq), 0) // CHUNK
        cc = lax.broadcasted_iota(jnp.int32, (tq, tq), 1) // CHUNK
        diag_ok = cc <= rc
        extra_ok = lax.broadcasted_iota(jnp.int32, (tq, LANE), 1) < n_valid_extra

        def fix(s):
            return jnp.concatenate([jnp.where(diag_ok, s[:, :tq], NEG), jnp.where(extra_ok, s[:, tq:], NEG)], axis=1)

        group(jnp.concatenate([k_diag, k_extra], axis=0), fix, tq, first=True)
        lax.fori_loop(0, i, body, 0)
    else:
        blk = min(rows, ATT_ROWS)
        n_groups = k_ref.shape[1] // ATT_GROUP
        for g in range(n_groups):
            valid = min(ATT_GROUP, n_valid_extra - g * ATT_GROUP)
            fix = None
            if valid < ATT_GROUP:
                ok = lax.broadcasted_iota(jnp.int32, (blk, ATT_GROUP), 1) < valid
                fix = lambda s, ok=ok: jnp.where(ok, s, NEG)
            group(k_ref[0, g * ATT_GROUP:(g + 1) * ATT_GROUP, :], fix, blk, first=g == 0)

    lat = acc_s[...] / l_s[...]
    lat_wide = jnp.concatenate([lat[hh * tq:(hh + 1) * tq] for hh in range(HEADS)], axis=1).astype(BF16)
    o_ref[0] = jnp.dot(lat_wide, wuv_ref[...], preferred_element_type=F32)


def _attention(q, keys, keys_extra, wuv, tq, n_main, n_valid_extra):
    B, _, T, _ = q.shape
    tkeys = keys.shape[1]
    kern = functools.partial(_attn_kernel, tq=tq, n_main=n_main, n_valid_extra=n_valid_extra)
    rows = HEADS * tq
    return pl.pallas_call(
        kern,
        grid=(B, T // tq),
        in_specs=[pl.BlockSpec((1, HEADS, tq, QK_W), lambda b, i: (b, 0, i, 0)),
                  pl.BlockSpec((1, tkeys, QK_W), lambda b, i: (b, 0, 0)),
                  _full(keys_extra.shape), _full(wuv.shape)],
        out_specs=pl.BlockSpec((1, tq, RW), lambda b, i: (b, i, 0)),
        out_shape=jax.ShapeDtypeStruct((B, T, RW), F32),
        scratch_shapes=[pltpu.VMEM((rows, LANE), F32)] * 3,
        compiler_params=pltpu.CompilerParams(dimension_semantics=("arbitrary", "arbitrary"),
                                             vmem_limit_bytes=VMEM_LIMIT),
        name="mla_attention",
    )(q, keys, keys_extra, wuv)


def _stage_d_kernel(x1_ref, y_ref, bon_ref, g_ref, mla_ref, ones_ref, lnw_ref, lnb_ref, wo_ref,
                    n2_ref, w1_ref, w3_ref, w2_ref, fn_ref, o_ref):
    y = y_ref[...]
    mu = _dot_x2(y, ones_ref[...]) * (1.0 / HD)
    d = y - mu
    var = _dot(d * d, ones_ref[...]) * (1.0 / HD)
    yn = d * lax.rsqrt(var + GN_EPS) * lnw_ref[...] + lnb_ref[...]
    rw = ((yn + bon_ref[...]) * g_ref[...]).astype(BF16)
    x2 = (x1_ref[...] + jnp.dot(rw, wo_ref[:RW, :], preferred_element_type=F32)
          + jnp.dot(mla_ref[...].astype(BF16), wo_ref[RW:, :], preferred_element_type=F32))
    h = _rms(x2, n2_ref[...]).astype(BF16)
    x3 = x2 + 0.5 * _ffn(h, w1_ref, w3_ref, w2_ref)
    o_ref[...] = _rms(x3, fn_ref[...])


def _stage_d(x1, y, bon, g, mla, W, tm):
    B, T, _ = x1.shape
    n = B * T
    flat = lambda t: t.reshape(n, t.shape[-1])
    tok = lambda w: pl.BlockSpec((tm, w), lambda i: (i, 0))
    weights = [W["ones"], W["lnw"], W["lnb"], W["wout"], W["n2"], W["f2w1"], W["f2w3"], W["f2w2"], W["fnorm"]]
    out = pl.pallas_call(
        _stage_d_kernel,
        grid=(n // tm,),
        in_specs=[tok(D_MODEL), tok(RW), tok(RW), tok(RW), tok(RW)] + [_full(w.shape) for w in weights],
        out_specs=tok(D_MODEL),
        out_shape=jax.ShapeDtypeStruct((n, D_MODEL), F32),
        compiler_params=pltpu.CompilerParams(dimension_semantics=("arbitrary",), vmem_limit_bytes=VMEM_LIMIT),
        name="stage_d",
    )(flat(x1), flat(y), flat(bon), flat(g), flat(mla), *weights)
    return out.reshape(B, T, D_MODEL)


def _rot_cols(w):
    half = ROPE // 2
    return jnp.concatenate([-w[..., half:], w[..., :half]], axis=-1)


def _prep_weights(norm_ffn1, ffn1_w1, ffn1_w3, ffn1_w2, norm_mix, w_in, mu_shift, w0, w_w2, a0, w_a2, w_g2, k_k,
                  k_a, r_k, ln_x_w, ln_x_b, q_norm, w_q_up, kv_norm, w_uk, w_uv, w_out, norm_ffn2, ffn2_w1,
                  ffn2_w3, ffn2_w2, final_norm):
    row = lambda t: t.reshape(1, -1).astype(F32)
    wi = w_in[0]
    wkr = wi[:, O_KR:O_KR + ROPE]
    win = jnp.concatenate([wi, _rot_cols(wkr), jnp.zeros((D_MODEL, LANE - 2 * ROPE), F32)], axis=1)
    zl = jnp.zeros((HD, RW), F32)
    wq = w_q_up[0]
    wq_rope = wq[:, :, HD:]
    wqr = jnp.concatenate([wq_rope, _rot_cols(wq_rope), jnp.zeros((Q_LORA, HEADS, LANE - 2 * ROPE), F32)], axis=-1)
    eye = jnp.eye(HEADS, dtype=F32)
    wuk_bd = jnp.einsum("chd,hg->hdgc", w_uk[0], eye).reshape(HEADS * HD, HEADS * KV_LORA)
    wuv_bd = jnp.einsum("chd,hg->hcgd", w_uv[0], eye).reshape(HEADS * KV_LORA, RW)
    ones_bd = jnp.kron(eye, jnp.ones((HD, HD), F32))
    return dict(
        n1=row(norm_ffn1), f1w1=ffn1_w1[0].astype(BF16), f1w3=ffn1_w3[0].astype(BF16), f1w2=ffn1_w2[0].astype(BF16),
        nmix=row(norm_mix), win=win.astype(BF16), mu=row(mu_shift),
        w0=row(w0), ww2=jnp.concatenate([w_w2[0], zl], axis=0).astype(BF16), a0=row(a0),
        wa2=jnp.concatenate([zl, w_a2[0]], axis=0).astype(BF16), wg2=w_g2[0].astype(BF16),
        kk=row(k_k), ka=row(k_a), rk=row(r_k), ones=ones_bd.astype(BF16),
        qn=row(q_norm), wqn=wq[:, :, :HD].reshape(Q_LORA, RW).astype(BF16),
        wqr=wqr.reshape(Q_LORA, HEADS * LANE).astype(BF16),
        kvn=row(kv_norm), wuk=wuk_bd.astype(BF16), wuv=wuv_bd.astype(BF16),
        lnw=row(ln_x_w), lnb=row(ln_x_b), wout=w_out[0].astype(BF16),
        n2=row(norm_ffn2), f2w1=ffn2_w1[0].astype(BF16), f2w3=ffn2_w3[0].astype(BF16),
        f2w2=ffn2_w2[0].astype(BF16), fnorm=row(final_norm))


def _rope_tables(pos):
    half = ROPE // 2
    inv = ROPE_THETA ** (-jnp.arange(half, dtype=F32) / half)
    ang = pos.astype(F32)[:, None] * inv[None, :]
    pad = lambda t: jnp.pad(jnp.concatenate([t, t], axis=1), ((0, 0), (0, LANE - ROPE)))
    return pad(jnp.cos(ang)), pad(jnp.sin(ang))


def _state_to_pairs(s):
    B = s.shape[0]
    s = s.reshape(B, N_PAIR, 2, HD, HD)
    z = jnp.zeros_like(s[:, :, 0])
    top = jnp.concatenate([s[:, :, 0], z], axis=-1)
    bot = jnp.concatenate([z, s[:, :, 1]], axis=-1)
    return jnp.concatenate([top, bot], axis=-2)


def _pairs_to_state(s):
    B = s.shape[0]
    return jnp.stack([s[:, :, :HD, :HD], s[:, :, HD:, HD:]], axis=2).reshape(B, HEADS, HD, HD)


def _pad_rows(t, n):
    return jnp.pad(t, ((0, 0), (0, n - t.shape[1]), (0, 0)))


def _tile(n, pref):
    return pref if n % pref == 0 else n


def kernel(x_prompt, x_sample, cache_ckv, cache_krope, state_wkv, state_shift, meta_tokens, norm_ffn1, ffn1_w1, ffn1_w3, ffn1_w2, norm_mix, w_in, mu_shift, w0, w_w2, a0, w_a2, w_g2, k_k, k_a, r_k, ln_x_w, ln_x_b, q_norm, w_q_up, kv_norm, w_uk, w_uv, w_out, norm_ffn2, ffn2_w1, ffn2_w3, ffn2_w2, final_norm):
    assert w_in.shape[0] == 1, "single-layer stack"
    B, S, _ = x_prompt.shape
    Bd, Tn, _ = x_sample.shape
    past = cache_ckv.shape[2]
    assert S % ATT_GROUP == 0 and Tn % SCAN == 0
    W = _prep_weights(norm_ffn1, ffn1_w1, ffn1_w3, ffn1_w2, norm_mix, w_in, mu_shift, w0, w_w2, a0, w_a2, w_g2,
                      k_k, k_a, r_k, ln_x_w, ln_x_b, q_norm, w_q_up, kv_norm, w_uk, w_uv, w_out, norm_ffn2,
                      ffn2_w1, ffn2_w3, ffn2_w2, final_norm)
    scan_keys = ("r", "lw", "k", "v", "a", "b")

    cos_m, sin_m = _rope_tables(jnp.arange(N_META))
    xm = meta_tokens[None].astype(F32)
    am = _stage_a(_ffn1(xm, W, N_META), jnp.zeros((1, 1, RWC), F32), cos_m, sin_m, W, N_META)
    _, st_m = _rwkv([_pad_rows(am[n], SCAN) for n in scan_keys], jnp.zeros((1, N_PAIR, PAIR, PAIR), F32))

    tm = _tile(S, 512)
    cos_p, sin_p = _rope_tables(N_META + jnp.arange(S))
    x1_p = _ffn1(x_prompt, W, _tile(B * S, 1024))
    ap = _stage_a(x1_p, jnp.broadcast_to(am["shift"], (B, 1, RWC)), cos_p, sin_p, W, tm)
    y_p, st_p = _rwkv([ap[n] for n in scan_keys], jnp.broadcast_to(st_m, (B, N_PAIR, PAIR, PAIR)))
    keys_m = _pad_rows(am["kcat"], LANE)[0]
    mla_p = _attention(ap["q"], ap["kcat"], keys_m, W["wuv"], ATT_GROUP, S, N_META)
    y_prompt = _stage_d(x1_p, y_p, ap["bon"], ap["g"], mla_p, W, tm)

    cos_s, sin_s = _rope_tables(N_META + past + jnp.arange(Tn))
    x1_s = _ffn1(x_sample, W, _tile(Bd * Tn, 512))
    as_ = _stage_a(x1_s, state_shift[0].astype(F32), cos_s, sin_s, W, Tn)
    y_s, st_s = _rwkv([as_[n] for n in scan_keys], _state_to_pairs(state_wkv[0].astype(F32)))
    cache_k = jnp.concatenate([cache_ckv[0], cache_krope[0], jnp.zeros((Bd, past, QK_W - KV_LORA - ROPE), F32)],
                              axis=-1).astype(BF16)
    n_valid = past + Tn + N_META
    n_keys = -(-n_valid // ATT_GROUP) * ATT_GROUP
    keys_s = jnp.concatenate([cache_k, as_["kcat"], jnp.broadcast_to(am["kcat"], (Bd, N_META, QK_W)),
                              jnp.zeros((Bd, n_keys - n_valid, QK_W), BF16)], axis=1)
    mla_s = _attention(as_["q"], keys_s, keys_m, W["wuv"], Tn, 0, n_valid)
    y_sample = _stage_d(x1_s, y_s, as_["bon"], as_["g"], mla_s, W, _tile(Bd * Tn, 512))

    bc = lambda t, n: jnp.broadcast_to(t, (n,) + t.shape[1:])
    ckv_p = jnp.concatenate([bc(am["c"], B), ap["c"]], axis=1)[None]
    kr_p = jnp.concatenate([bc(am["kr"], B), ap["kr"]], axis=1)[None]
    return (y_prompt, y_sample, ckv_p, kr_p, _pairs_to_state(st_p)[None], ap["shift"][None],
            as_["c"][None], as_["kr"][None], _pairs_to_state(st_s)[None], as_["shift"][None])
```

```python
import functools

import jax
import jax.numpy as jnp
from jax import lax
from jax.experimental import pallas as pl
from jax.experimental.pallas import tpu as pltpu

F32 = jnp.float32
BF16 = jnp.bfloat16

D_MODEL = 1024
D_FF = 2816
CHUNK = 64
N_META = 16
HEADS = 8
HD = 64
RW = HEADS * HD
ROPE = 32
Q_LORA = 256
KV_LORA = 128
NORM_EPS = 1e-6
GN_EPS = 64e-5
MLA_SCALE = (HD + ROPE) ** -0.5
ROPE_THETA = 10000.0

LANE = 128
SCAN = 64
PAIR = 2 * HD
N_PAIR = HEADS // 2
RWKV_ROWS = 4
QK_W = 256

O_R, O_K, O_V, O_WA, O_GL = 0, 512, 1024, 1536, 1664
RWC = 1792
O_Q, O_KV, O_KR = 1792, 2048, 2176
PW = 2304
ROT_SHIFT = LANE - ROPE
FF_CHUNK = 256
VMEM_LIMIT = 56 * 1024 * 1024
NEG = -1e30
LOG2E = 1.4426950408889634
ATT_ROWS = 1024
ATT_GROUP = 512


def _dot(a, b):
    return jnp.dot(a.astype(BF16), b.astype(BF16), preferred_element_type=F32)


def _dot_nt(a, b):
    return lax.dot_general(a.astype(BF16), b.astype(BF16), (((1,), (1,)), ((), ())),
                           preferred_element_type=F32)


def _split2(x):
    hi = x.astype(BF16)
    lo = (x - hi.astype(F32)).astype(BF16)
    return hi, lo


def _dot_x2(a, b):
    hi, lo = _split2(a)
    bb = b.astype(BF16)
    return (jnp.dot(hi, bb, preferred_element_type=F32) + jnp.dot(lo, bb, preferred_element_type=F32))


def _head_sums(x, ones_ref, dot=_dot):
    w = ones_ref.shape[0]
    return jnp.concatenate([dot(x[:, j * w:(j + 1) * w], ones_ref[...]) for j in range(RW // w)], axis=1)


def _rms(x, g):
    return x * lax.rsqrt(jnp.mean(x * x, axis=-1, keepdims=True) + NORM_EPS) * g


def _ffn(h, w1_ref, w3_ref, w2_ref):
    acc = None
    for c in range(D_FF // FF_CHUNK):
        sl = slice(c * FF_CHUNK, (c + 1) * FF_CHUNK)
        a = jnp.dot(h, w1_ref[:, sl], preferred_element_type=F32)
        b = jnp.dot(h, w3_ref[:, sl], preferred_element_type=F32)
        z = (a * jax.nn.sigmoid(a) * b).astype(BF16)
        t = jnp.dot(z, w2_ref[sl, :], preferred_element_type=F32)
        acc = t if acc is None else acc + t
    return acc


def _ffn1_kernel(x_ref, n1_ref, w1_ref, w3_ref, w2_ref, o_ref):
    x = x_ref[...]
    h = _rms(x, n1_ref[...]).astype(BF16)
    o_ref[...] = x + 0.5 * _ffn(h, w1_ref, w3_ref, w2_ref)


def _full(shape):
    return pl.BlockSpec(memory_space=pltpu.VMEM)


def _ffn1(x, W, tm):
    B, T, _ = x.shape
    n = B * T
    tok = pl.BlockSpec((tm, D_MODEL), lambda i: (i, 0))
    weights = [W["n1"], W["f1w1"], W["f1w3"], W["f1w2"]]
    out = pl.pallas_call(
        _ffn1_kernel,
        grid=(n // tm,),
        in_specs=[tok] + [_full(w.shape) for w in weights],
        out_specs=tok,
        out_shape=jax.ShapeDtypeStruct((n, D_MODEL), F32),
        compiler_params=pltpu.CompilerParams(dimension_semantics=("arbitrary",), vmem_limit_bytes=VMEM_LIMIT),
        name="ffn1",
    )(x.reshape(n, D_MODEL), *weights)
    return out.reshape(B, T, D_MODEL)


def _rope(blk, cos, sin):
    return blk * cos + pltpu.roll(blk, ROT_SHIFT, 1) * sin


def _stage_a_kernel(x1_ref, sh0_ref, cos_ref, sin_ref,
                    nmix_ref, win_ref, mu_ref,
                    w0_ref, ww2_ref, a0_ref, wa2_ref, wg2_ref, kk_ref, ka_ref, rk_ref, ones_ref,
                    qn_ref, wqn_ref, wqr_ref, kvn_ref, wuk_ref,
                    r_o, lw_o, k_o, v_o, a_o, b_o, bon_o, g_o, c_o, kr_o, kcat_o, q_o, sh_o,
                    prev_s, *, tm, n_sub):
    t = pl.program_id(1)

    @pl.when(t == 0)
    def _():
        prev_s[...] = sh0_ref[0]

    ts = tm // n_sub

    def rows_block(rows, prev_row):
        hm = _rms(x1_ref[0, rows, :], nmix_ref[...]).astype(BF16)
        p = jnp.dot(hm, win_ref[...], preferred_element_type=F32)

        prw = p[:, :RWC]
        rolled = pltpu.roll(prw, 1, 0)
        row = lax.broadcasted_iota(jnp.int32, (8, RWC), 0)
        prev = jnp.concatenate([jnp.where(row == 0, prev_row, rolled[:8]), rolled[8:]], axis=0)
        s = prw + mu_ref[...] * (prev - prw)

        r = s[:, O_R:O_R + RW]
        k = s[:, O_K:O_K + RW]
        v = s[:, O_V:O_V + RW]
        wa = s[:, O_WA:O_WA + LANE]
        gl = s[:, O_GL:O_GL + LANE]
        z = w0_ref[...] + _dot(jnp.tanh(wa), ww2_ref[...])
        softplus = jnp.maximum(-z, 0.0) + jnp.log(1.0 + jnp.exp(-jnp.abs(z)))
        logw = -softplus - 0.5
        a_sig = jax.nn.sigmoid(a0_ref[...] + _dot(wa, wa2_ref[...]))
        kk = k * kk_ref[...]
        ss = _head_sums(kk * kk, ones_ref)
        kk = kk * lax.rsqrt(jnp.maximum(ss, 1e-24))
        k2 = k * (1.0 + (a_sig - 1.0) * ka_ref[...])

        cos = cos_ref[rows, :]
        sin = sin_ref[rows, :]
        c = _rms(p[:, O_KV:O_KV + KV_LORA], kvn_ref[...])
        kr = _rope(p[:, O_KR:O_KR + LANE], cos, sin)
        cq = _rms(p[:, O_Q:O_Q + Q_LORA], qn_ref[...]).astype(BF16)
        qn = jnp.dot(cq, wqn_ref[...], preferred_element_type=F32)
        qn16 = qn.astype(BF16)
        qlat = jnp.concatenate([jnp.dot(qn16[:, j * PAIR:(j + 1) * PAIR], wuk_ref[j], preferred_element_type=F32)
                                for j in range(N_PAIR)], axis=1)
        qr = jnp.dot(cq, wqr_ref[...], preferred_element_type=F32)
        q = []
        for hh in range(HEADS):
            sl = slice(hh * LANE, (hh + 1) * LANE)
            qrh = _rope(qr[:, sl], cos, sin)
            q.append((jnp.concatenate([qlat[:, sl], qrh], axis=1) * (MLA_SCALE * LOG2E)).astype(BF16))
        outs = dict(r=r, lw=-jnp.exp(logw), k=k2, v=v.astype(BF16), a=-kk, b=kk * a_sig,
                    bon=_head_sums(r * k2 * rk_ref[...], ones_ref) * v, g=_dot(jax.nn.sigmoid(gl), wg2_ref[...]),
                    c=c, kr=kr[:, :ROPE], kcat=jnp.concatenate([c, kr], axis=1).astype(BF16), q=q)
        return outs, prw[ts - 1:ts, :]

    prev_row = prev_s[...]
    done = []
    for i in range(n_sub):
        rows = slice(i * ts, (i + 1) * ts)
        outs, prev_row = rows_block(rows, prev_row)
        done.append((rows, outs))
    prev_s[...] = prev_row
    sh_o[0] = prev_row
    refs = dict(r=r_o, lw=lw_o, k=k_o, v=v_o, a=a_o, b=b_o, bon=bon_o, g=g_o, c=c_o, kr=kr_o, kcat=kcat_o)
    for rows, outs in done:
        for name, ref in refs.items():
            ref[0, rows, :] = outs[name]
        for hh in range(HEADS):
            q_o[0, hh, rows, :] = outs["q"][hh]


def _stage_a(x1, sh0, cos, sin, W, tm):
    B, T, _ = x1.shape
    nt = T // tm
    tok = lambda w: pl.BlockSpec((1, tm, w), lambda b, t: (b, t, 0))
    weights = [W["nmix"], W["win"], W["mu"],
               W["w0"], W["ww2"], W["a0"], W["wa2"], W["wg2"], W["kk"], W["ka"], W["rk"], W["ones"],
               W["qn"], W["wqn"], W["wqr"], W["kvn"], W["wuk"]]
    in_specs = ([tok(D_MODEL), pl.BlockSpec((1, 1, RWC), lambda b, t: (b, 0, 0)),
                 pl.BlockSpec((tm, LANE), lambda b, t: (t, 0)), pl.BlockSpec((tm, LANE), lambda b, t: (t, 0))]
                + [_full(w.shape) for w in weights])
    f = lambda w, dt=F32: jax.ShapeDtypeStruct((B, T, w), dt)
    names = ["r", "lw", "k", "v", "a", "b", "bon", "g", "c", "kr", "kcat", "q", "shift"]
    out_shape = [f(RW), f(RW), f(RW), f(RW, BF16), f(RW), f(RW), f(RW), f(RW), f(KV_LORA), f(ROPE), f(QK_W, BF16),
                 jax.ShapeDtypeStruct((B, HEADS, T, QK_W), BF16), jax.ShapeDtypeStruct((B, 1, RWC), F32)]
    out_specs = ([tok(RW)] * 8 + [tok(KV_LORA), tok(ROPE), tok(QK_W),
                                  pl.BlockSpec((1, HEADS, tm, QK_W), lambda b, t: (b, 0, t, 0)),
                                  pl.BlockSpec((1, 1, RWC), lambda b, t: (b, 0, 0))])
    outs = pl.pallas_call(
        functools.partial(_stage_a_kernel, tm=tm, n_sub=2 if tm % 512 == 0 else 1),
        grid=(B, nt), in_specs=in_specs, out_specs=out_specs, out_shape=out_shape,
        scratch_shapes=[pltpu.VMEM((1, RWC), F32)],
        compiler_params=pltpu.CompilerParams(dimension_semantics=("arbitrary", "arbitrary"),
                                             vmem_limit_bytes=VMEM_LIMIT),
        name="stage_a",
    )(x1, sh0, cos, sin, *weights)
    return dict(zip(names, outs))


def _stack(x, lo_mask):
    return jnp.concatenate([jnp.where(lo_mask, x, 0.0), jnp.where(lo_mask, 0.0, x)], axis=0)


def _rwkv_kernel(r_ref, lw_ref, k_ref, v_ref, a_ref, b_ref, s0_ref, y_o, st_o, st_s, *, nb):
    c = pl.program_id(1)

    @pl.when(c == 0)
    def _():
        st_s[...] = s0_ref[...]

    n2 = 2 * SCAN
    ti = lax.broadcasted_iota(jnp.int32, (SCAN, SCAN), 0)
    tj = lax.broadcasted_iota(jnp.int32, (SCAN, SCAN), 1)
    tri = jnp.where(ti >= tj, 1.0, 0.0).astype(BF16)
    ri = lax.broadcasted_iota(jnp.int32, (n2, n2), 0) % SCAN
    ci = lax.broadcasted_iota(jnp.int32, (n2, n2), 1) % SCAN
    strict = ri > ci
    incl = ri >= ci
    lo_mask = lax.broadcasted_iota(jnp.int32, (SCAN, PAIR), 1) < HD

    chains = [(i, j) for i in range(nb) for j in range(N_PAIR)]
    sl = lambda j: slice(j * PAIR, (j + 1) * PAIR)
    sts = [st_s[i, j] for i, j in chains]
    ins = [[ref[i, :, sl(j)] for ref in (r_ref, k_ref, v_ref, a_ref, b_ref)] for i, j in chains]

    cums, lws = [], []
    for i in range(nb):
        lw_all = lw_ref[i]
        l1 = lw_all.astype(BF16)
        d1 = lw_all - l1.astype(F32)
        l2 = d1.astype(BF16)
        l3 = (d1 - l2.astype(F32)).astype(BF16)
        cum_all = (jnp.dot(tri, l1, preferred_element_type=F32) + jnp.dot(tri, l2, preferred_element_type=F32)
                   + jnp.dot(tri, l3, preferred_element_type=F32))
        for j in range(N_PAIR):
            cums.append(cum_all[:, sl(j)])
            lws.append(lw_all[:, sl(j)])

    def prep(n):
        r, k, v, a, b = ins[n]
        cum = cums[n]
        cum_c = cum[SCAN - 1:SCAN, :]
        e_n = jnp.exp(-cum)
        e_c = jnp.exp(cum_c - cum)
        return dict(
            p_c=jnp.exp(cum_c),
            ls=_stack(a * jnp.exp(cum - lws[n]), lo_mask).astype(BF16),
            rs=_stack(r * jnp.exp(cum), lo_mask),
            bs=_stack(b * e_n, lo_mask).astype(BF16), ks=_stack(k * e_n, lo_mask).astype(BF16),
            vs=_stack(v.astype(F32), lo_mask).astype(BF16),
            bh=_stack(b * e_c, lo_mask).astype(BF16), kh=_stack(k * e_c, lo_mask).astype(BF16))

    P = [prep(n) for n in range(len(chains))]
    aas = [_dot_nt(jnp.concatenate([p["ls"], p["rs"].astype(BF16)], axis=0),
                   jnp.concatenate([p["bs"], p["ks"]], axis=0)) for p in P]
    pws = [jnp.where(strict, aa[:n2, :n2], 0.0) for aa in aas]
    aks = [jnp.where(strict, aa[:n2, n2:], 0.0) for aa in aas]
    rbs = [jnp.where(incl, aa[n2:, :n2], 0.0) for aa in aas]
    rks = [jnp.where(incl, aa[n2:, n2:], 0.0) for aa in aas]

    xs = [jnp.concatenate([p["ls"].astype(F32), _dot(ak, p["vs"])], axis=1) for p, ak in zip(P, aks)]
    def live_rows(m, z):
        return m if z == 0 else jnp.concatenate([m[z:SCAN], m[SCAN + z:]], axis=0)

    def put_rows(full, part, z):
        if z == 0:
            return part
        h = SCAN - z
        return jnp.concatenate([full[:z], part[:h], full[SCAN:SCAN + z], part[h:]], axis=0)

    for it in range(6):
        z = (1 << it) if (1 << it) % 16 == 0 else 0
        xs = [put_rows(x, live_rows(x, z) + _dot(live_rows(pw, z), x), z) for x, pw in zip(xs, pws)]
        if it < 5:
            z2 = (2 << it) if (2 << it) % 16 == 0 else 0
            zero = jnp.zeros((n2, n2), F32)
            pws = [put_rows(zero, _dot(live_rows(pw, z2), pw), z2) for pw in pws]

    wls = [x[:, :PAIR] for x in xs]
    uvs = [jnp.concatenate([x[:, PAIR:], p["vs"].astype(F32)], axis=0) for x, p in zip(xs, P)]
    rqs = [p["rs"] + _dot(rb, wl) for p, rb, wl in zip(P, rbs, wls)]
    y0s = [_dot(jnp.concatenate([rb, rk], axis=1), uv) for rb, rk, uv in zip(rbs, rks, uvs)]
    gms = [_dot(wl.T, p["bh"]) for wl, p in zip(wls, P)]
    hms = [_dot(uv.T, jnp.concatenate([p["bh"], p["kh"]], axis=0)) for uv, p in zip(uvs, P)]
    yss = [_dot_nt(rq, st) + y0 for rq, st, y0 in zip(rqs, sts, y0s)]
    ys_out = [ys[:SCAN] + ys[SCAN:] for ys in yss]
    st_new = [p["p_c"] * st + _dot(st, g_m) + h_m for p, st, g_m, h_m in zip(P, sts, gms, hms)]

    for i in range(nb):
        y_o[i] = jnp.concatenate(ys_out[i * N_PAIR:(i + 1) * N_PAIR], axis=1)
    for n, (i, j) in enumerate(chains):
        st_s[i, j] = st_new[n]
        st_o[i, j] = st_new[n]


def _rwkv(ins, s0):
    B, T, _ = ins[0].shape
    nc = T // SCAN
    nb = RWKV_ROWS if B % RWKV_ROWS == 0 else 1
    tok = pl.BlockSpec((nb, SCAN, RW), lambda b, c: (b, c, 0))
    st_spec = pl.BlockSpec((nb, N_PAIR, PAIR, PAIR), lambda b, c: (b, 0, 0, 0))
    return pl.pallas_call(
        functools.partial(_rwkv_kernel, nb=nb),
        grid=(B // nb, nc), in_specs=[tok] * 6 + [st_spec], out_specs=[tok, st_spec],
        out_shape=[jax.ShapeDtypeStruct((B, T, RW), F32), jax.ShapeDtypeStruct((B, N_PAIR, PAIR, PAIR), F32)],
        scratch_shapes=[pltpu.VMEM((nb, N_PAIR, PAIR, PAIR), F32)],
        compiler_params=pltpu.CompilerParams(dimension_semantics=("arbitrary", "arbitrary"),
                                             vmem_limit_bytes=VMEM_LIMIT),
        name="rwkv_scan",
    )(*ins, s0)


def _attn_kernel(q_ref, k_ref, kx_ref, wuv_ref, o_ref, m_s, l_s, acc_s, *, tq, n_main, n_valid_extra):
    i = pl.program_id(1)
    rows = HEADS * tq
    q = q_ref[0].reshape(rows, QK_W)
    def group(kb, fix, blk, first=False):
        vb = kb[:, :KV_LORA]
        reps = kb.shape[0] // LANE

        def chain(rb):
            s = lax.dot_general(q[rb], kb, (((1,), (1,)), ((), ())), preferred_element_type=F32)
            if fix is not None:
                s = fix(s)
            yield
            m_cur = jnp.max(s, axis=1, keepdims=True)
            if first:
                m_next = jnp.broadcast_to(m_cur, (blk, LANE))
            else:
                m_prev = m_s[rb, :]
                m_next = jnp.maximum(m_prev, m_cur)
                alpha = jnp.exp2(m_prev - m_next)
            p = jnp.exp2(s - jnp.concatenate([m_next] * reps, axis=1))
            l_new = jnp.broadcast_to(jnp.sum(p, axis=1, keepdims=True), (blk, LANE))
            if not first:
                l_new = alpha * l_s[rb, :] + l_new
            yield
            acc_new = jnp.dot(p.astype(BF16), vb, preferred_element_type=F32)
            if not first:
                acc_new = alpha * acc_s[rb, :] + acc_new
            return rb, m_next, l_new, acc_new

        pending = [slice(j * blk, (j + 1) * blk) for j in range(rows // blk)]
        running, done = [], []
        while pending or running:
            if pending:
                running.append(chain(pending.pop(0)))
            for gen in list(running):
                try:
                    next(gen)
                except StopIteration as stop:
                    done.append(stop.value)
                    running.remove(gen)
        for rb, m_next, l_new, acc_new in done:
            m_s[rb, :] = m_next
            l_s[rb, :] = l_new
            acc_s[rb, :] = acc_new

    if n_main:
        def body(g, carry):
            group(k_ref[0, pl.ds(pl.multiple_of(g * tq, tq), tq), :], None, min(rows, ATT_ROWS))
            return carry

        k_diag = k_ref[0, pl.ds(pl.multiple_of(i * tq, tq), tq), :]
        k_extra = kx_ref[...]
        rc = lax.broadcasted_iota(jnp.int32, (tq, tq), 0) // CHUNK
        cc = lax.broadcasted_iota(jnp.int32, (tq, tq), 1) // CHUNK
        diag_ok = cc <= rc
        extra_ok = lax.broadcasted_iota(jnp.int32, (tq, LANE), 1) < n_valid_extra

        def fix(s):
            return jnp.concatenate([jnp.where(diag_ok, s[:, :tq], NEG), jnp.where(extra_ok, s[:, tq:], NEG)], axis=1)

        group(jnp.concatenate([k_diag, k_extra], axis=0), fix, tq, first=True)
        lax.fori_loop(0, i, body, 0)
    else:
        blk = min(rows, ATT_ROWS)
        n_groups = k_ref.shape[1] // ATT_GROUP
        for g in range(n_groups):
            valid = min(ATT_GROUP, n_valid_extra - g * ATT_GROUP)
            fix = None
            if valid < ATT_GROUP:
                ok = lax.broadcasted_iota(jnp.int32, (blk, ATT_GROUP), 1) < valid
                fix = lambda s, ok=ok: jnp.where(ok, s, NEG)
            group(k_ref[0, g * ATT_GROUP:(g + 1) * ATT_GROUP, :], fix, blk, first=g == 0)

    lat = acc_s[...] / l_s[...]
    heads = [lat[hh * tq:(hh + 1) * tq].astype(BF16) for hh in range(HEADS)]
    o_ref[0] = jnp.concatenate([jnp.dot(jnp.concatenate(heads[2 * j:2 * j + 2], axis=1), wuv_ref[j],
                                        preferred_element_type=F32) for j in range(N_PAIR)], axis=1)


def _attention(q, keys, keys_extra, wuv, tq, n_main, n_valid_extra):
    B, _, T, _ = q.shape
    tkeys = keys.shape[1]
    kern = functools.partial(_attn_kernel, tq=tq, n_main=n_main, n_valid_extra=n_valid_extra)
    rows = HEADS * tq
    return pl.pallas_call(
        kern,
        grid=(B, T // tq),
        in_specs=[pl.BlockSpec((1, HEADS, tq, QK_W), lambda b, i: (b, 0, i, 0)),
                  pl.BlockSpec((1, tkeys, QK_W), lambda b, i: (b, 0, 0)),
                  _full(keys_extra.shape), _full(wuv.shape)],
        out_specs=pl.BlockSpec((1, tq, RW), lambda b, i: (b, i, 0)),
        out_shape=jax.ShapeDtypeStruct((B, T, RW), F32),
        scratch_shapes=[pltpu.VMEM((rows, LANE), F32)] * 3,
        compiler_params=pltpu.CompilerParams(dimension_semantics=("arbitrary", "arbitrary"),
                                             vmem_limit_bytes=VMEM_LIMIT),
        name="mla_attention",
    )(q, keys, keys_extra, wuv)


def _stage_d_kernel(x1_ref, y_ref, bon_ref, g_ref, mla_ref, ones_ref, lnw_ref, lnb_ref, wo_ref,
                    n2_ref, w1_ref, w3_ref, w2_ref, fn_ref, o_ref):
    y = y_ref[...]
    mu = _head_sums(y, ones_ref, _dot_x2) * (1.0 / HD)
    d = y - mu
    var = _head_sums(d * d, ones_ref) * (1.0 / HD)
    yn = d * lax.rsqrt(var + GN_EPS) * lnw_ref[...] + lnb_ref[...]
    rw = ((yn + bon_ref[...]) * g_ref[...]).astype(BF16)
    x2 = (x1_ref[...] + jnp.dot(rw, wo_ref[:RW, :], preferred_element_type=F32)
          + jnp.dot(mla_ref[...].astype(BF16), wo_ref[RW:, :], preferred_element_type=F32))
    h = _rms(x2, n2_ref[...]).astype(BF16)
    x3 = x2 + 0.5 * _ffn(h, w1_ref, w3_ref, w2_ref)
    o_ref[...] = _rms(x3, fn_ref[...])


def _stage_d(x1, y, bon, g, mla, W, tm):
    B, T, _ = x1.shape
    n = B * T
    flat = lambda t: t.reshape(n, t.shape[-1])
    tok = lambda w: pl.BlockSpec((tm, w), lambda i: (i, 0))
    weights = [W["ones"], W["lnw"], W["lnb"], W["wout"], W["n2"], W["f2w1"], W["f2w3"], W["f2w2"], W["fnorm"]]
    out = pl.pallas_call(
        _stage_d_kernel,
        grid=(n // tm,),
        in_specs=[tok(D_MODEL), tok(RW), tok(RW), tok(RW), tok(RW)] + [_full(w.shape) for w in weights],
        out_specs=tok(D_MODEL),
        out_shape=jax.ShapeDtypeStruct((n, D_MODEL), F32),
        compiler_params=pltpu.CompilerParams(dimension_semantics=("arbitrary",), vmem_limit_bytes=VMEM_LIMIT),
        name="stage_d",
    )(flat(x1), flat(y), flat(bon), flat(g), flat(mla), *weights)
    return out.reshape(B, T, D_MODEL)


def _rot_cols(w):
    half = ROPE // 2
    return jnp.concatenate([-w[..., half:], w[..., :half]], axis=-1)


def _prep_weights(norm_ffn1, ffn1_w1, ffn1_w3, ffn1_w2, norm_mix, w_in, mu_shift, w0, w_w2, a0, w_a2, w_g2, k_k,
                  k_a, r_k, ln_x_w, ln_x_b, q_norm, w_q_up, kv_norm, w_uk, w_uv, w_out, norm_ffn2, ffn2_w1,
                  ffn2_w3, ffn2_w2, final_norm):
    row = lambda t: t.reshape(1, -1).astype(F32)
    wi = w_in[0]
    wkr = wi[:, O_KR:O_KR + ROPE]
    win = jnp.concatenate([wi, _rot_cols(wkr), jnp.zeros((D_MODEL, LANE - 2 * ROPE), F32)], axis=1)
    zl = jnp.zeros((HD, RW), F32)
    wq = w_q_up[0]
    wq_rope = wq[:, :, HD:]
    wqr = jnp.concatenate([wq_rope, _rot_cols(wq_rope), jnp.zeros((Q_LORA, HEADS, LANE - 2 * ROPE), F32)], axis=-1)
    assert win.shape[1] == PW
    eye2 = jnp.eye(2, dtype=F32)
    wuk_bd = jnp.einsum("cjhd,hg->jhdgc", w_uk[0].reshape(KV_LORA, N_PAIR, 2, HD), eye2).reshape(N_PAIR, PAIR, 2 * KV_LORA)
    wuv_bd = jnp.einsum("cjhd,hg->jhcgd", w_uv[0].reshape(KV_LORA, N_PAIR, 2, HD), eye2).reshape(N_PAIR, 2 * KV_LORA, PAIR)
    ones_bd = jnp.kron(jnp.eye(HEADS // 2, dtype=F32), jnp.ones((HD, HD), F32))
    return dict(
        n1=row(norm_ffn1), f1w1=ffn1_w1[0].astype(BF16), f1w3=ffn1_w3[0].astype(BF16), f1w2=ffn1_w2[0].astype(BF16),
        nmix=row(norm_mix), win=win.astype(BF16), mu=row(mu_shift),
        w0=row(w0), ww2=jnp.concatenate([w_w2[0], zl], axis=0).astype(BF16), a0=row(a0),
        wa2=jnp.concatenate([zl, w_a2[0]], axis=0).astype(BF16), wg2=w_g2[0].astype(BF16),
        kk=row(k_k), ka=row(k_a), rk=row(r_k), ones=ones_bd.astype(BF16),
        qn=row(q_norm), wqn=wq[:, :, :HD].reshape(Q_LORA, RW).astype(BF16),
        wqr=wqr.reshape(Q_LORA, HEADS * LANE).astype(BF16),
        kvn=row(kv_norm), wuk=wuk_bd.astype(BF16), wuv=wuv_bd.astype(BF16),
        lnw=row(ln_x_w), lnb=row(ln_x_b), wout=w_out[0].astype(BF16),
        n2=row(norm_ffn2), f2w1=ffn2_w1[0].astype(BF16), f2w3=ffn2_w3[0].astype(BF16),
        f2w2=ffn2_w2[0].astype(BF16), fnorm=row(final_norm))


def _rope_tables(pos):
    half = ROPE // 2
    inv = ROPE_THETA ** (-jnp.arange(half, dtype=F32) / half)
    ang = pos.astype(F32)[:, None] * inv[None, :]
    pad = lambda t: jnp.pad(jnp.concatenate([t, t], axis=1), ((0, 0), (0, LANE - ROPE)))
    return pad(jnp.cos(ang)), pad(jnp.sin(ang))


def _state_to_pairs(s):
    B = s.shape[0]
    s = s.reshape(B, N_PAIR, 2, HD, HD)
    z = jnp.zeros_like(s[:, :, 0])
    top = jnp.concatenate([s[:, :, 0], z], axis=-1)
    bot = jnp.concatenate([z, s[:, :, 1]], axis=-1)
    return jnp.concatenate([top, bot], axis=-2)


def _pairs_to_state(s):
    B = s.shape[0]
    return jnp.stack([s[:, :, :HD, :HD], s[:, :, HD:, HD:]], axis=2).reshape(B, HEADS, HD, HD)


def _pad_rows(t, n):
    return jnp.pad(t, ((0, 0), (0, n - t.shape[1]), (0, 0)))


def _tile(n, pref):
    return pref if n % pref == 0 else n


def kernel(x_prompt, x_sample, cache_ckv, cache_krope, state_wkv, state_shift, meta_tokens, norm_ffn1, ffn1_w1, ffn1_w3, ffn1_w2, norm_mix, w_in, mu_shift, w0, w_w2, a0, w_a2, w_g2, k_k, k_a, r_k, ln_x_w, ln_x_b, q_norm, w_q_up, kv_norm, w_uk, w_uv, w_out, norm_ffn2, ffn2_w1, ffn2_w3, ffn2_w2, final_norm):
    assert w_in.shape[0] == 1, "single-layer stack"
    B, S, _ = x_prompt.shape
    Bd, Tn, _ = x_sample.shape
    past = cache_ckv.shape[2]
    assert S % ATT_GROUP == 0 and Tn % SCAN == 0
    W = _prep_weights(norm_ffn1, ffn1_w1, ffn1_w3, ffn1_w2, norm_mix, w_in, mu_shift, w0, w_w2, a0, w_a2, w_g2,
                      k_k, k_a, r_k, ln_x_w, ln_x_b, q_norm, w_q_up, kv_norm, w_uk, w_uv, w_out, norm_ffn2,
                      ffn2_w1, ffn2_w3, ffn2_w2, final_norm)
    scan_keys = ("r", "lw", "k", "v", "a", "b")

    cos_m, sin_m = _rope_tables(jnp.arange(N_META))
    xm = meta_tokens[None].astype(F32)
    am = _stage_a(_ffn1(xm, W, N_META), jnp.zeros((1, 1, RWC), F32), cos_m, sin_m, W, N_META)
    _, st_m = _rwkv([_pad_rows(am[n], SCAN) for n in scan_keys], jnp.zeros((1, N_PAIR, PAIR, PAIR), F32))

    tm = _tile(S, 512)
    cos_p, sin_p = _rope_tables(N_META + jnp.arange(S))
    x1_p = _ffn1(x_prompt, W, _tile(B * S, 1024))
    ap = _stage_a(x1_p, jnp.broadcast_to(am["shift"], (B, 1, RWC)), cos_p, sin_p, W, tm)
    y_p, st_p = _rwkv([ap[n] for n in scan_keys], jnp.broadcast_to(st_m, (B, N_PAIR, PAIR, PAIR)))
    keys_m = _pad_rows(am["kcat"], LANE)[0]
    mla_p = _attention(ap["q"], ap["kcat"], keys_m, W["wuv"], ATT_GROUP, S, N_META)
    y_prompt = _stage_d(x1_p, y_p, ap["bon"], ap["g"], mla_p, W, tm)

    cos_s, sin_s = _rope_tables(N_META + past + jnp.arange(Tn))
    x1_s = _ffn1(x_sample, W, _tile(Bd * Tn, 512))
    as_ = _stage_a(x1_s, state_shift[0].astype(F32), cos_s, sin_s, W, Tn)
    y_s, st_s = _rwkv([as_[n] for n in scan_keys], _state_to_pairs(state_wkv[0].astype(F32)))
    cache_k = jnp.concatenate([cache_ckv[0], cache_krope[0], jnp.zeros((Bd, past, QK_W - KV_LORA - ROPE), F32)],
                              axis=-1).astype(BF16)
    n_valid = past + Tn + N_META
    n_keys = -(-n_valid // ATT_GROUP) * ATT_GROUP
    keys_s = jnp.concatenate([cache_k, as_["kcat"], jnp.broadcast_to(am["kcat"], (Bd, N_META, QK_W)),
                              jnp.zeros((Bd, n_keys - n_valid, QK_W), BF16)], axis=1)
    mla_s = _attention(as_["q"], keys_s, keys_m, W["wuv"], Tn, 0, n_valid)
    y_sample = _stage_d(x1_s, y_s, as_["bon"], as_["g"], mla_s, W, _tile(Bd * Tn, 512))

    bc = lambda t, n: jnp.broadcast_to(t, (n,) + t.shape[1:])
    ckv_p = jnp.concatenate([bc(am["c"], B), ap["c"]], axis=1)[None]
    kr_p = jnp.concatenate([bc(am["kr"], B), ap["kr"]], axis=1)[None]
    return (y_prompt, y_sample, ckv_p, kr_p, _pairs_to_state(st_p)[None], ap["shift"][None],
            as_["c"][None], as_["kr"][None], _pairs_to_state(st_s)[None], as_["shift"][None])
```

```python
import functools

import jax
import jax.numpy as jnp
from jax import lax
from jax.experimental import pallas as pl
from jax.experimental.pallas import tpu as pltpu

F32 = jnp.float32
BF16 = jnp.bfloat16

D_MODEL = 1024
D_FF = 2816
CHUNK = 64
N_META = 16
HEADS = 8
HD = 64
RW = HEADS * HD
ROPE = 32
Q_LORA = 256
KV_LORA = 128
NORM_EPS = 1e-6
GN_EPS = 64e-5
MLA_SCALE = (HD + ROPE) ** -0.5
ROPE_THETA = 10000.0

LANE = 128
SCAN = 64
PAIR = 2 * HD
N_PAIR = HEADS // 2
RWKV_ROWS = 8
QK_W = 256

O_R, O_K, O_V, O_WA, O_GL = 0, 512, 1024, 1536, 1664
RWC = 1792
O_Q, O_KV, O_KR = 1792, 2048, 2176
PW = 2304
ROT_SHIFT = LANE - ROPE
FF_CHUNK = 256
VMEM_LIMIT = 56 * 1024 * 1024
NEG = -1e30
LOG2E = 1.4426950408889634
ATT_ROWS = 1024
ATT_GROUP = 512


def _dot(a, b):
    return jnp.dot(a.astype(BF16), b.astype(BF16), preferred_element_type=F32)


def _dot_nt(a, b):
    return lax.dot_general(a.astype(BF16), b.astype(BF16), (((1,), (1,)), ((), ())),
                           preferred_element_type=F32)


def _split2(x):
    hi = x.astype(BF16)
    lo = (x - hi.astype(F32)).astype(BF16)
    return hi, lo


def _dot_x2(a, b):
    hi, lo = _split2(a)
    bb = b.astype(BF16)
    return (jnp.dot(hi, bb, preferred_element_type=F32) + jnp.dot(lo, bb, preferred_element_type=F32))


def _head_sums(x, ones_ref, dot=_dot):
    w = ones_ref.shape[0]
    return jnp.concatenate([dot(x[:, j * w:(j + 1) * w], ones_ref[...]) for j in range(RW // w)], axis=1)


def _rms(x, g):
    return x * lax.rsqrt(jnp.mean(x * x, axis=-1, keepdims=True) + NORM_EPS) * g


def _ffn(h, w1_ref, w3_ref, w2_ref):
    acc = None
    for c in range(D_FF // FF_CHUNK):
        sl = slice(c * FF_CHUNK, (c + 1) * FF_CHUNK)
        a = jnp.dot(h, w1_ref[:, sl], preferred_element_type=F32)
        b = jnp.dot(h, w3_ref[:, sl], preferred_element_type=F32)
        z = (a * jax.nn.sigmoid(a) * b).astype(BF16)
        t = jnp.dot(z, w2_ref[sl, :], preferred_element_type=F32)
        acc = t if acc is None else acc + t
    return acc


def _ffn1_kernel(x_ref, n1_ref, w1_ref, w3_ref, w2_ref, o_ref):
    x = x_ref[...]
    h = _rms(x, n1_ref[...]).astype(BF16)
    o_ref[...] = x + 0.5 * _ffn(h, w1_ref, w3_ref, w2_ref)


def _full(shape):
    return pl.BlockSpec(memory_space=pltpu.VMEM)


def _ffn1(x, W, tm):
    B, T, _ = x.shape
    n = B * T
    tok = pl.BlockSpec((tm, D_MODEL), lambda i: (i, 0))
    weights = [W["n1"], W["f1w1"], W["f1w3"], W["f1w2"]]
    out = pl.pallas_call(
        _ffn1_kernel,
        grid=(n // tm,),
        in_specs=[tok] + [_full(w.shape) for w in weights],
        out_specs=tok,
        out_shape=jax.ShapeDtypeStruct((n, D_MODEL), F32),
        compiler_params=pltpu.CompilerParams(dimension_semantics=("arbitrary",), vmem_limit_bytes=VMEM_LIMIT),
        name="ffn1",
    )(x.reshape(n, D_MODEL), *weights)
    return out.reshape(B, T, D_MODEL)


def _rope(blk, cos, sin):
    return blk * cos + pltpu.roll(blk, ROT_SHIFT, 1) * sin


def _stage_a_kernel(x1_ref, sh0_ref, cos_ref, sin_ref,
                    nmix_ref, win_ref, mu_ref,
                    w0_ref, ww2_ref, a0_ref, wa2_ref, wg2_ref, kk_ref, ka_ref, rk_ref, ones_ref,
                    qn_ref, wqn_ref, wqr_ref, kvn_ref, wuk_ref,
                    r_o, lw_o, k_o, v_o, a_o, b_o, bon_o, g_o, c_o, kr_o, kcat_o, q_o, sh_o,
                    prev_s, *, tm, n_sub):
    t = pl.program_id(1)

    @pl.when(t == 0)
    def _():
        prev_s[...] = sh0_ref[0]

    ts = tm // n_sub

    def rows_block(rows, prev_row):
        hm = _rms(x1_ref[0, rows, :], nmix_ref[...]).astype(BF16)
        p = jnp.dot(hm, win_ref[...], preferred_element_type=F32)

        prw = p[:, :RWC]
        rolled = pltpu.roll(prw, 1, 0)
        row = lax.broadcasted_iota(jnp.int32, (8, RWC), 0)
        prev = jnp.concatenate([jnp.where(row == 0, prev_row, rolled[:8]), rolled[8:]], axis=0)
        s = prw + mu_ref[...] * (prev - prw)

        r = s[:, O_R:O_R + RW]
        k = s[:, O_K:O_K + RW]
        v = s[:, O_V:O_V + RW]
        wa = s[:, O_WA:O_WA + LANE]
        gl = s[:, O_GL:O_GL + LANE]
        z = w0_ref[...] + _dot(jnp.tanh(wa), ww2_ref[...])
        softplus = jnp.maximum(-z, 0.0) + jnp.log(1.0 + jnp.exp(-jnp.abs(z)))
        logw = -softplus - 0.5
        a_sig = jax.nn.sigmoid(a0_ref[...] + _dot(wa, wa2_ref[...]))
        kk = k * kk_ref[...]
        ss = _head_sums(kk * kk, ones_ref)
        kk = kk * lax.rsqrt(jnp.maximum(ss, 1e-24))
        k2 = k * (1.0 + (a_sig - 1.0) * ka_ref[...])

        cos = cos_ref[rows, :]
        sin = sin_ref[rows, :]
        c = _rms(p[:, O_KV:O_KV + KV_LORA], kvn_ref[...])
        kr = _rope(p[:, O_KR:O_KR + LANE], cos, sin)
        cq = _rms(p[:, O_Q:O_Q + Q_LORA], qn_ref[...]).astype(BF16)
        qn = jnp.dot(cq, wqn_ref[...], preferred_element_type=F32)
        qn16 = qn.astype(BF16)
        qlat = jnp.concatenate([jnp.dot(qn16[:, j * PAIR:(j + 1) * PAIR], wuk_ref[j], preferred_element_type=F32)
                                for j in range(N_PAIR)], axis=1)
        qr = jnp.dot(cq, wqr_ref[...], preferred_element_type=F32)
        q = []
        for hh in range(HEADS):
            sl = slice(hh * LANE, (hh + 1) * LANE)
            qrh = _rope(qr[:, sl], cos, sin)
            q.append((jnp.concatenate([qlat[:, sl], qrh], axis=1) * (MLA_SCALE * LOG2E)).astype(BF16))
        outs = dict(r=r, lw=-jnp.exp(logw), k=k2, v=v.astype(BF16), a=-kk, b=kk * a_sig,
                    bon=_head_sums(r * k2 * rk_ref[...], ones_ref) * v, g=_dot(jax.nn.sigmoid(gl), wg2_ref[...]),
                    c=c, kr=kr[:, :ROPE], kcat=jnp.concatenate([c, kr], axis=1).astype(BF16), q=q)
        return outs, prw[ts - 1:ts, :]

    prev_row = prev_s[...]
    done = []
    for i in range(n_sub):
        rows = slice(i * ts, (i + 1) * ts)
        outs, prev_row = rows_block(rows, prev_row)
        done.append((rows, outs))
    prev_s[...] = prev_row
    sh_o[0] = prev_row
    refs = dict(r=r_o, lw=lw_o, k=k_o, v=v_o, a=a_o, b=b_o, bon=bon_o, g=g_o, c=c_o, kr=kr_o, kcat=kcat_o)
    for rows, outs in done:
        for name, ref in refs.items():
            ref[0, rows, :] = outs[name]
        for hh in range(HEADS):
            q_o[0, hh, rows, :] = outs["q"][hh]


def _stage_a(x1, sh0, cos, sin, W, tm):
    B, T, _ = x1.shape
    nt = T // tm
    tok = lambda w: pl.BlockSpec((1, tm, w), lambda b, t: (b, t, 0))
    weights = [W["nmix"], W["win"], W["mu"],
               W["w0"], W["ww2"], W["a0"], W["wa2"], W["wg2"], W["kk"], W["ka"], W["rk"], W["ones"],
               W["qn"], W["wqn"], W["wqr"], W["kvn"], W["wuk"]]
    in_specs = ([tok(D_MODEL), pl.BlockSpec((1, 1, RWC), lambda b, t: (b, 0, 0)),
                 pl.BlockSpec((tm, LANE), lambda b, t: (t, 0)), pl.BlockSpec((tm, LANE), lambda b, t: (t, 0))]
                + [_full(w.shape) for w in weights])
    f = lambda w, dt=F32: jax.ShapeDtypeStruct((B, T, w), dt)
    names = ["r", "lw", "k", "v", "a", "b", "bon", "g", "c", "kr", "kcat", "q", "shift"]
    out_shape = [f(RW), f(RW), f(RW), f(RW, BF16), f(RW), f(RW), f(RW), f(RW), f(KV_LORA), f(ROPE), f(QK_W, BF16),
                 jax.ShapeDtypeStruct((B, HEADS, T, QK_W), BF16), jax.ShapeDtypeStruct((B, 1, RWC), F32)]
    out_specs = ([tok(RW)] * 8 + [tok(KV_LORA), tok(ROPE), tok(QK_W),
                                  pl.BlockSpec((1, HEADS, tm, QK_W), lambda b, t: (b, 0, t, 0)),
                                  pl.BlockSpec((1, 1, RWC), lambda b, t: (b, 0, 0))])
    outs = pl.pallas_call(
        functools.partial(_stage_a_kernel, tm=tm, n_sub=2 if tm % 512 == 0 else 1),
        grid=(B, nt), in_specs=in_specs, out_specs=out_specs, out_shape=out_shape,
        scratch_shapes=[pltpu.VMEM((1, RWC), F32)],
        compiler_params=pltpu.CompilerParams(dimension_semantics=("arbitrary", "arbitrary"),
                                             vmem_limit_bytes=VMEM_LIMIT),
        name="stage_a",
    )(x1, sh0, cos, sin, *weights)
    return dict(zip(names, outs))


def _stack(x, lo_mask):
    return jnp.concatenate([jnp.where(lo_mask, x, 0.0), jnp.where(lo_mask, 0.0, x)], axis=0)


def _rwkv_kernel(r_ref, lw_ref, k_ref, v_ref, a_ref, b_ref, s0_ref, y_o, st_o, st_s, *, nb):
    c = pl.program_id(1)

    @pl.when(c == 0)
    def _():
        st_s[...] = s0_ref[...]

    n2 = 2 * SCAN
    ti = lax.broadcasted_iota(jnp.int32, (SCAN, SCAN), 0)
    tj = lax.broadcasted_iota(jnp.int32, (SCAN, SCAN), 1)
    tri = jnp.where(ti >= tj, 1.0, 0.0).astype(BF16)
    ri = lax.broadcasted_iota(jnp.int32, (n2, n2), 0) % SCAN
    ci = lax.broadcasted_iota(jnp.int32, (n2, n2), 1) % SCAN
    strict = ri > ci
    incl = ri >= ci
    lo_mask = lax.broadcasted_iota(jnp.int32, (SCAN, PAIR), 1) < HD

    chains = [(i, j) for i in range(nb) for j in range(N_PAIR)]
    sl = lambda j: slice(j * PAIR, (j + 1) * PAIR)
    sts = [st_s[i, j] for i, j in chains]
    ins = [[ref[i, :, sl(j)] for ref in (r_ref, k_ref, v_ref, a_ref, b_ref)] for i, j in chains]

    cums, lws = [], []
    for i in range(nb):
        lw_all = lw_ref[i]
        l1 = lw_all.astype(BF16)
        d1 = lw_all - l1.astype(F32)
        l2 = d1.astype(BF16)
        l3 = (d1 - l2.astype(F32)).astype(BF16)
        cum_all = (jnp.dot(tri, l1, preferred_element_type=F32) + jnp.dot(tri, l2, preferred_element_type=F32)
                   + jnp.dot(tri, l3, preferred_element_type=F32))
        for j in range(N_PAIR):
            cums.append(cum_all[:, sl(j)])
            lws.append(lw_all[:, sl(j)])

    def prep(n):
        r, k, v, a, b = ins[n]
        cum = cums[n]
        cum_c = cum[SCAN - 1:SCAN, :]
        e_n = jnp.exp(-cum)
        e_c = jnp.exp(cum_c - cum)
        return dict(
            p_c=jnp.exp(cum_c),
            ls=_stack(a * jnp.exp(cum - lws[n]), lo_mask).astype(BF16),
            rs=_stack(r * jnp.exp(cum), lo_mask),
            bs=_stack(b * e_n, lo_mask).astype(BF16), ks=_stack(k * e_n, lo_mask).astype(BF16),
            vs=_stack(v.astype(F32), lo_mask).astype(BF16),
            bh=_stack(b * e_c, lo_mask).astype(BF16), kh=_stack(k * e_c, lo_mask).astype(BF16))

    P = [prep(n) for n in range(len(chains))]
    aas = [_dot_nt(jnp.concatenate([p["ls"], p["rs"].astype(BF16)], axis=0),
                   jnp.concatenate([p["bs"], p["ks"]], axis=0)) for p in P]
    pws = [jnp.where(strict, aa[:n2, :n2], 0.0) for aa in aas]
    aks = [jnp.where(strict, aa[:n2, n2:], 0.0) for aa in aas]
    rbs = [jnp.where(incl, aa[n2:, :n2], 0.0) for aa in aas]
    rks = [jnp.where(incl, aa[n2:, n2:], 0.0) for aa in aas]

    xs = [jnp.concatenate([p["ls"].astype(F32), _dot(ak, p["vs"])], axis=1) for p, ak in zip(P, aks)]
    def live_rows(m, z):
        return m if z == 0 else jnp.concatenate([m[z:SCAN], m[SCAN + z:]], axis=0)

    def put_rows(full, part, z):
        if z == 0:
            return part
        h = SCAN - z
        return jnp.concatenate([full[:z], part[:h], full[SCAN:SCAN + z], part[h:]], axis=0)

    for it in range(6):
        z = (1 << it) if (1 << it) % 16 == 0 else 0
        xs = [put_rows(x, live_rows(x, z) + _dot(live_rows(pw, z), x), z) for x, pw in zip(xs, pws)]
        if it < 5:
            z2 = (2 << it) if (2 << it) % 16 == 0 else 0
            zero = jnp.zeros((n2, n2), F32)
            pws = [put_rows(zero, _dot(live_rows(pw, z2), pw), z2) for pw in pws]

    wls = [x[:, :PAIR] for x in xs]
    uvs = [jnp.concatenate([x[:, PAIR:], p["vs"].astype(F32)], axis=0) for x, p in zip(xs, P)]
    rqs = [p["rs"] + _dot(rb, wl) for p, rb, wl in zip(P, rbs, wls)]
    y0s = [_dot(jnp.concatenate([rb, rk], axis=1), uv) for rb, rk, uv in zip(rbs, rks, uvs)]
    gms = [_dot(wl.T, p["bh"]) for wl, p in zip(wls, P)]
    hms = [_dot(uv.T, jnp.concatenate([p["bh"], p["kh"]], axis=0)) for uv, p in zip(uvs, P)]
    yss = [_dot_nt(rq, st) + y0 for rq, st, y0 in zip(rqs, sts, y0s)]
    own = jnp.concatenate([lo_mask, jnp.logical_not(lo_mask)], axis=0)
    mus = [jnp.sum(ys, axis=1, keepdims=True) * (1.0 / HD) for ys in yss]
    ds = [jnp.where(own, ys - mu, 0.0) for ys, mu in zip(yss, mus)]
    vrs = [jnp.sum(d * d, axis=1, keepdims=True) * (1.0 / HD) for d in ds]
    yns = [d * lax.rsqrt(vr + GN_EPS) for d, vr in zip(ds, vrs)]
    ys_out = [yn[:SCAN] + yn[SCAN:] for yn in yns]
    st_new = [p["p_c"] * st + _dot(st, g_m) + h_m for p, st, g_m, h_m in zip(P, sts, gms, hms)]

    for i in range(nb):
        y_o[i] = jnp.concatenate(ys_out[i * N_PAIR:(i + 1) * N_PAIR], axis=1)
    for n, (i, j) in enumerate(chains):
        st_s[i, j] = st_new[n]
        st_o[i, j] = st_new[n]


def _rwkv(ins, s0):
    B, T, _ = ins[0].shape
    nc = T // SCAN
    nb = RWKV_ROWS if B % RWKV_ROWS == 0 else 1
    tok = pl.BlockSpec((nb, SCAN, RW), lambda b, c: (b, c, 0))
    st_spec = pl.BlockSpec((nb, N_PAIR, PAIR, PAIR), lambda b, c: (b, 0, 0, 0))
    return pl.pallas_call(
        functools.partial(_rwkv_kernel, nb=nb),
        grid=(B // nb, nc), in_specs=[tok] * 6 + [st_spec], out_specs=[tok, st_spec],
        out_shape=[jax.ShapeDtypeStruct((B, T, RW), F32), jax.ShapeDtypeStruct((B, N_PAIR, PAIR, PAIR), F32)],
        scratch_shapes=[pltpu.VMEM((nb, N_PAIR, PAIR, PAIR), F32)],
        compiler_params=pltpu.CompilerParams(dimension_semantics=("arbitrary", "arbitrary"),
                                             vmem_limit_bytes=VMEM_LIMIT),
        name="rwkv_scan",
    )(*ins, s0)


def _attn_kernel(q_ref, k_ref, kx_ref, wuv_ref, o_ref, m_s, l_s, acc_s, *, tq, n_main, n_valid_extra):
    i = pl.program_id(1)
    rows = HEADS * tq
    q = q_ref[0].reshape(rows, QK_W)
    def group(kb, fix, blk, first=False):
        vb = kb[:, :KV_LORA]
        reps = kb.shape[0] // LANE

        def chain(rb):
            s = lax.dot_general(q[rb], kb, (((1,), (1,)), ((), ())), preferred_element_type=F32)
            if fix is not None:
                s = fix(s)
            yield
            m_cur = jnp.max(s, axis=1, keepdims=True)
            if first:
                m_next = jnp.broadcast_to(m_cur, (blk, LANE))
            else:
                m_prev = m_s[rb, :]
                m_next = jnp.maximum(m_prev, m_cur)
                alpha = jnp.exp2(m_prev - m_next)
            p = jnp.exp2(s - jnp.concatenate([m_next] * reps, axis=1))
            l_new = jnp.broadcast_to(jnp.sum(p, axis=1, keepdims=True), (blk, LANE))
            if not first:
                l_new = alpha * l_s[rb, :] + l_new
            yield
            acc_new = jnp.dot(p.astype(BF16), vb, preferred_element_type=F32)
            if not first:
                acc_new = alpha * acc_s[rb, :] + acc_new
            return rb, m_next, l_new, acc_new

        pending = [slice(j * blk, (j + 1) * blk) for j in range(rows // blk)]
        running, done = [], []
        while pending or running:
            if pending:
                running.append(chain(pending.pop(0)))
            for gen in list(running):
                try:
                    next(gen)
                except StopIteration as stop:
                    done.append(stop.value)
                    running.remove(gen)
        for rb, m_next, l_new, acc_new in done:
            m_s[rb, :] = m_next
            l_s[rb, :] = l_new
            acc_s[rb, :] = acc_new

    if n_main:
        def body(g, carry):
            group(k_ref[0, pl.ds(pl.multiple_of(g * tq, tq), tq), :], None, min(rows, ATT_ROWS))
            return carry

        k_diag = k_ref[0, pl.ds(pl.multiple_of(i * tq, tq), tq), :]
        k_extra = kx_ref[...]
        rc = lax.broadcasted_iota(jnp.int32, (tq, tq), 0) // CHUNK
        cc = lax.broadcasted_iota(jnp.int32, (tq, tq), 1) // CHUNK
        diag_ok = cc <= rc
        extra_ok = lax.broadcasted_iota(jnp.int32, (tq, LANE), 1) < n_valid_extra

        def fix(s):
            return jnp.concatenate([jnp.where(diag_ok, s[:, :tq], NEG), jnp.where(extra_ok, s[:, tq:], NEG)], axis=1)

        group(jnp.concatenate([k_diag, k_extra], axis=0), fix, tq, first=True)
        lax.fori_loop(0, i, body, 0)
    else:
        blk = min(rows, ATT_ROWS)
        n_groups = k_ref.shape[1] // ATT_GROUP
        for g in range(n_groups):
            valid = min(ATT_GROUP, n_valid_extra - g * ATT_GROUP)
            fix = None
            if valid < ATT_GROUP:
                ok = lax.broadcasted_iota(jnp.int32, (blk, ATT_GROUP), 1) < valid
                fix = lambda s, ok=ok: jnp.where(ok, s, NEG)
            group(k_ref[0, g * ATT_GROUP:(g + 1) * ATT_GROUP, :], fix, blk, first=g == 0)

    lat = acc_s[...] / l_s[...]
    heads = [lat[hh * tq:(hh + 1) * tq].astype(BF16) for hh in range(HEADS)]
    o_ref[0] = jnp.concatenate([jnp.dot(jnp.concatenate(heads[2 * j:2 * j + 2], axis=1), wuv_ref[j],
                                        preferred_element_type=F32) for j in range(N_PAIR)], axis=1)


def _attention(q, keys, keys_extra, wuv, tq, n_main, n_valid_extra):
    B, _, T, _ = q.shape
    tkeys = keys.shape[1]
    kern = functools.partial(_attn_kernel, tq=tq, n_main=n_main, n_valid_extra=n_valid_extra)
    rows = HEADS * tq
    return pl.pallas_call(
        kern,
        grid=(B, T // tq),
        in_specs=[pl.BlockSpec((1, HEADS, tq, QK_W), lambda b, i: (b, 0, i, 0)),
                  pl.BlockSpec((1, tkeys, QK_W), lambda b, i: (b, 0, 0)),
                  _full(keys_extra.shape), _full(wuv.shape)],
        out_specs=pl.BlockSpec((1, tq, RW), lambda b, i: (b, i, 0)),
        out_shape=jax.ShapeDtypeStruct((B, T, RW), F32),
        scratch_shapes=[pltpu.VMEM((rows, LANE), F32)] * 3,
        compiler_params=pltpu.CompilerParams(dimension_semantics=("arbitrary", "arbitrary"),
                                             vmem_limit_bytes=VMEM_LIMIT),
        name="mla_attention",
    )(q, keys, keys_extra, wuv)


def _stage_d_kernel(x1_ref, y_ref, bon_ref, g_ref, mla_ref, lnw_ref, lnb_ref, wo_ref,
                    n2_ref, w1_ref, w3_ref, w2_ref, fn_ref, o_ref):
    yn = y_ref[...] * lnw_ref[...] + lnb_ref[...]
    rw = ((yn + bon_ref[...]) * g_ref[...]).astype(BF16)
    x2 = (x1_ref[...] + jnp.dot(rw, wo_ref[:RW, :], preferred_element_type=F32)
          + jnp.dot(mla_ref[...].astype(BF16), wo_ref[RW:, :], preferred_element_type=F32))
    h = _rms(x2, n2_ref[...]).astype(BF16)
    x3 = x2 + 0.5 * _ffn(h, w1_ref, w3_ref, w2_ref)
    o_ref[...] = _rms(x3, fn_ref[...])


def _stage_d(x1, y, bon, g, mla, W, tm):
    B, T, _ = x1.shape
    n = B * T
    flat = lambda t: t.reshape(n, t.shape[-1])
    tok = lambda w: pl.BlockSpec((tm, w), lambda i: (i, 0))
    weights = [W["lnw"], W["lnb"], W["wout"], W["n2"], W["f2w1"], W["f2w3"], W["f2w2"], W["fnorm"]]
    out = pl.pallas_call(
        _stage_d_kernel,
        grid=(n // tm,),
        in_specs=[tok(D_MODEL), tok(RW), tok(RW), tok(RW), tok(RW)] + [_full(w.shape) for w in weights],
        out_specs=tok(D_MODEL),
        out_shape=jax.ShapeDtypeStruct((n, D_MODEL), F32),
        compiler_params=pltpu.CompilerParams(dimension_semantics=("arbitrary",), vmem_limit_bytes=VMEM_LIMIT),
        name="stage_d",
    )(flat(x1), flat(y), flat(bon), flat(g), flat(mla), *weights)
    return out.reshape(B, T, D_MODEL)


def _rot_cols(w):
    half = ROPE // 2
    return jnp.concatenate([-w[..., half:], w[..., :half]], axis=-1)


def _prep_weights(norm_ffn1, ffn1_w1, ffn1_w3, ffn1_w2, norm_mix, w_in, mu_shift, w0, w_w2, a0, w_a2, w_g2, k_k,
                  k_a, r_k, ln_x_w, ln_x_b, q_norm, w_q_up, kv_norm, w_uk, w_uv, w_out, norm_ffn2, ffn2_w1,
                  ffn2_w3, ffn2_w2, final_norm):
    row = lambda t: t.reshape(1, -1).astype(F32)
    wi = w_in[0]
    wkr = wi[:, O_KR:O_KR + ROPE]
    win = jnp.concatenate([wi, _rot_cols(wkr), jnp.zeros((D_MODEL, LANE - 2 * ROPE), F32)], axis=1)
    zl = jnp.zeros((HD, RW), F32)
    wq = w_q_up[0]
    wq_rope = wq[:, :, HD:]
    wqr = jnp.concatenate([wq_rope, _rot_cols(wq_rope), jnp.zeros((Q_LORA, HEADS, LANE - 2 * ROPE), F32)], axis=-1)
    assert win.shape[1] == PW
    eye2 = jnp.eye(2, dtype=F32)
    wuk_bd = jnp.einsum("cjhd,hg->jhdgc", w_uk[0].reshape(KV_LORA, N_PAIR, 2, HD), eye2).reshape(N_PAIR, PAIR, 2 * KV_LORA)
    wuv_bd = jnp.einsum("cjhd,hg->jhcgd", w_uv[0].reshape(KV_LORA, N_PAIR, 2, HD), eye2).reshape(N_PAIR, 2 * KV_LORA, PAIR)
    ones_bd = jnp.kron(jnp.eye(HEADS // 2, dtype=F32), jnp.ones((HD, HD), F32))
    return dict(
        n1=row(norm_ffn1), f1w1=ffn1_w1[0].astype(BF16), f1w3=ffn1_w3[0].astype(BF16), f1w2=ffn1_w2[0].astype(BF16),
        nmix=row(norm_mix), win=win.astype(BF16), mu=row(mu_shift),
        w0=row(w0), ww2=jnp.concatenate([w_w2[0], zl], axis=0).astype(BF16), a0=row(a0),
        wa2=jnp.concatenate([zl, w_a2[0]], axis=0).astype(BF16), wg2=w_g2[0].astype(BF16),
        kk=row(k_k), ka=row(k_a), rk=row(r_k), ones=ones_bd.astype(BF16),
        qn=row(q_norm), wqn=wq[:, :, :HD].reshape(Q_LORA, RW).astype(BF16),
        wqr=wqr.reshape(Q_LORA, HEADS * LANE).astype(BF16),
        kvn=row(kv_norm), wuk=wuk_bd.astype(BF16), wuv=wuv_bd.astype(BF16),
        lnw=row(ln_x_w), lnb=row(ln_x_b), wout=w_out[0].astype(BF16),
        n2=row(norm_ffn2), f2w1=ffn2_w1[0].astype(BF16), f2w3=ffn2_w3[0].astype(BF16),
        f2w2=ffn2_w2[0].astype(BF16), fnorm=row(final_norm))


def _rope_tables(pos):
    half = ROPE // 2
    inv = ROPE_THETA ** (-jnp.arange(half, dtype=F32) / half)
    ang = pos.astype(F32)[:, None] * inv[None, :]
    pad = lambda t: jnp.pad(jnp.concatenate([t, t], axis=1), ((0, 0), (0, LANE - ROPE)))
    return pad(jnp.cos(ang)), pad(jnp.sin(ang))


def _state_to_pairs(s):
    B = s.shape[0]
    s = s.reshape(B, N_PAIR, 2, HD, HD)
    z = jnp.zeros_like(s[:, :, 0])
    top = jnp.concatenate([s[:, :, 0], z], axis=-1)
    bot = jnp.concatenate([z, s[:, :, 1]], axis=-1)
    return jnp.concatenate([top, bot], axis=-2)


def _pairs_to_state(s):
    B = s.shape[0]
    return jnp.stack([s[:, :, :HD, :HD], s[:, :, HD:, HD:]], axis=2).reshape(B, HEADS, HD, HD)


def _pad_rows(t, n):
    return jnp.pad(t, ((0, 0), (0, n - t.shape[1]), (0, 0)))


def _tile(n, pref):
    return pref if n % pref == 0 else n


def kernel(x_prompt, x_sample, cache_ckv, cache_krope, state_wkv, state_shift, meta_tokens, norm_ffn1, ffn1_w1, ffn1_w3, ffn1_w2, norm_mix, w_in, mu_shift, w0, w_w2, a0, w_a2, w_g2, k_k, k_a, r_k, ln_x_w, ln_x_b, q_norm, w_q_up, kv_norm, w_uk, w_uv, w_out, norm_ffn2, ffn2_w1, ffn2_w3, ffn2_w2, final_norm):
    assert w_in.shape[0] == 1, "single-layer stack"
    B, S, _ = x_prompt.shape
    Bd, Tn, _ = x_sample.shape
    past = cache_ckv.shape[2]
    assert S % ATT_GROUP == 0 and Tn % SCAN == 0
    W = _prep_weights(norm_ffn1, ffn1_w1, ffn1_w3, ffn1_w2, norm_mix, w_in, mu_shift, w0, w_w2, a0, w_a2, w_g2,
                      k_k, k_a, r_k, ln_x_w, ln_x_b, q_norm, w_q_up, kv_norm, w_uk, w_uv, w_out, norm_ffn2,
                      ffn2_w1, ffn2_w3, ffn2_w2, final_norm)
    scan_keys = ("r", "lw", "k", "v", "a", "b")

    cos_m, sin_m = _rope_tables(jnp.arange(N_META))
    xm = meta_tokens[None].astype(F32)
    am = _stage_a(_ffn1(xm, W, N_META), jnp.zeros((1, 1, RWC), F32), cos_m, sin_m, W, N_META)
    _, st_m = _rwkv([_pad_rows(am[n], SCAN) for n in scan_keys], jnp.zeros((1, N_PAIR, PAIR, PAIR), F32))

    tm = _tile(S, 512)
    cos_p, sin_p = _rope_tables(N_META + jnp.arange(S))
    x1_p = _ffn1(x_prompt, W, _tile(B * S, 1024))
    ap = _stage_a(x1_p, jnp.broadcast_to(am["shift"], (B, 1, RWC)), cos_p, sin_p, W, tm)
    y_p, st_p = _rwkv([ap[n] for n in scan_keys], jnp.broadcast_to(st_m, (B, N_PAIR, PAIR, PAIR)))
    keys_m = _pad_rows(am["kcat"], LANE)[0]
    mla_p = _attention(ap["q"], ap["kcat"], keys_m, W["wuv"], ATT_GROUP, S, N_META)
    y_prompt = _stage_d(x1_p, y_p, ap["bon"], ap["g"], mla_p, W, tm)

    cos_s, sin_s = _rope_tables(N_META + past + jnp.arange(Tn))
    x1_s = _ffn1(x_sample, W, _tile(Bd * Tn, 512))
    as_ = _stage_a(x1_s, state_shift[0].astype(F32), cos_s, sin_s, W, Tn)
    y_s, st_s = _rwkv([as_[n] for n in scan_keys], _state_to_pairs(state_wkv[0].astype(F32)))
    cache_k = jnp.concatenate([cache_ckv[0], cache_krope[0], jnp.zeros((Bd, past, QK_W - KV_LORA - ROPE), F32)],
                              axis=-1).astype(BF16)
    n_valid = past + Tn + N_META
    n_keys = -(-n_valid // ATT_GROUP) * ATT_GROUP
    keys_s = jnp.concatenate([cache_k, as_["kcat"], jnp.broadcast_to(am["kcat"], (Bd, N_META, QK_W)),
                              jnp.zeros((Bd, n_keys - n_valid, QK_W), BF16)], axis=1)
    mla_s = _attention(as_["q"], keys_s, keys_m, W["wuv"], Tn, 0, n_valid)
    y_sample = _stage_d(x1_s, y_s, as_["bon"], as_["g"], mla_s, W, _tile(Bd * Tn, 512))

    bc = lambda t, n: jnp.broadcast_to(t, (n,) + t.shape[1:])
    ckv_p = jnp.concatenate([bc(am["c"], B), ap["c"]], axis=1)[None]
    kr_p = jnp.concatenate([bc(am["kr"], B), ap["kr"]], axis=1)[None]
    return (y_prompt, y_sample, ckv_p, kr_p, _pairs_to_state(st_p)[None], ap["shift"][None],
            as_["c"][None], as_["kr"][None], _pairs_to_state(st_s)[None], as_["shift"][None])
```
